```python
import jax, jax.numpy as jnp
from jax import lax
import numpy as np

D_MODEL = 2048
BATCH = 2
SEQ = 4096
DEPTH = 2
DEC_BATCH = 16
DEC_SEQ = 64
PAST_LEN = 2048

CHUNK = 64
N_HEADS = 16
N_KV_HEADS = 4
HEAD_DIM = 64
GROUP = N_HEADS // N_KV_HEADS
WINDOW = 128
BAND_CHUNKS = WINDOW // CHUNK + 1
ROT_DIM = HEAD_DIM // 4
ROPE_THETA = 500000.0
ATTN_SCALE = HEAD_DIM ** -0.5
POOL_WINDOWS = (2, 4, 8, 16)
N_POOL_GROUPS = 4
POOL_W = 1024
POOL_GROUP_W = POOL_W // N_POOL_GROUPS
POOL_HIST = max(POOL_WINDOWS) - 1
ATTN_W = N_HEADS * HEAD_DIM
KV_W = N_KV_HEADS * HEAD_DIM
IN_W = ATTN_W + 2 * KV_W + POOL_W + 2 * D_MODEL
SPLIT_AT = (ATTN_W, ATTN_W + KV_W, ATTN_W + 2 * KV_W, ATTN_W + 2 * KV_W + POOL_W,
            ATTN_W + 2 * KV_W + POOL_W + D_MODEL)
D_FF = 5632
N_EXPERTS = 8
TOP_K = 2
D_FF_EXPERT = 5632
PLE_DIM = 256
N_DENSE = (DEPTH + 1) // 2
N_MOE = DEPTH // 2
RMS_EPS = 1e-6
NEG_INF = -1e30

kernel_name = "hybrid_swa_pool_stream_encoder_step"


def rms_norm(x, g):
    xf = x.astype(jnp.float32)
    y = xf * lax.rsqrt(jnp.mean(xf * xf, axis=-1, keepdims=True) + RMS_EPS)
    return (y * g.astype(jnp.float32)).astype(x.dtype)


def apply_partial_rope(x, pos):
    inv = ROPE_THETA ** (-jnp.arange(0, ROT_DIM, 2, dtype=jnp.float32) / ROT_DIM)
    ang = pos.astype(jnp.float32)[:, None] * inv[None, :]
    cos = jnp.concatenate([jnp.cos(ang), jnp.cos(ang)], -1)[None, :, None, :]
    sin = jnp.concatenate([jnp.sin(ang), jnp.sin(ang)], -1)[None, :, None, :]
    xr = x[..., :ROT_DIM].astype(jnp.float32)
    x1, x2 = xr[..., :ROT_DIM // 2], xr[..., ROT_DIM // 2:]
    rot = xr * cos + jnp.concatenate([-x2, x1], -1) * sin
    return jnp.concatenate([rot.astype(x.dtype), x[..., ROT_DIM:]], -1)


def sink_attention(q, k, v, valid, sinks_l):
    s = jnp.einsum('bncxgd,bnlxd->bnxgcl', q, k).astype(jnp.float32) * ATTN_SCALE
    s = jnp.where(valid[None, :, None, None, None, :], s, NEG_INF)
    sink = jnp.broadcast_to(sinks_l.astype(jnp.float32).reshape(1, 1, N_KV_HEADS, GROUP, 1, 1),
                            s.shape[:-1] + (1,))
    p = jax.nn.softmax(jnp.concatenate([s, sink], axis=-1), axis=-1)[..., :-1].astype(v.dtype)
    return jnp.einsum('bnxgcl,bnlxd->bncxgd', p, v)


def banded_window_attention(q, k, v, sinks_l):
    b, s = q.shape[:2]
    nc = s // CHUNK
    qc = q.reshape(b, nc, CHUNK, N_KV_HEADS, GROUP, HEAD_DIM)
    pad = ((0, 0), ((BAND_CHUNKS - 1) * CHUNK, 0), (0, 0), (0, 0))
    kp = jnp.pad(k, pad).reshape(b, nc + BAND_CHUNKS - 1, CHUNK, N_KV_HEADS, HEAD_DIM)
    vp = jnp.pad(v, pad).reshape(b, nc + BAND_CHUNKS - 1, CHUNK, N_KV_HEADS, HEAD_DIM)
    kb = jnp.concatenate([kp[:, j:j + nc] for j in range(BAND_CHUNKS)], axis=2)
    vb = jnp.concatenate([vp[:, j:j + nc] for j in range(BAND_CHUNKS)], axis=2)
    key_chunk = (jnp.arange(nc)[:, None] + jnp.arange(BAND_CHUNKS * CHUNK)[None, :] // CHUNK
                 - (BAND_CHUNKS - 1))
    o = sink_attention(qc, kb, vb, key_chunk >= 0, sinks_l)
    return o.reshape(b, s, ATTN_W)


def pool_mixer(u, hist, pos0, pool_w_l, pool_scale_l):
    b, L, _ = u.shape
    ext = jnp.concatenate([hist, u], axis=1).astype(jnp.float32)
    cs = jnp.concatenate([jnp.zeros((b, 1, POOL_W), jnp.float32), jnp.cumsum(ext, axis=1)], axis=1)
    pos = pos0 + jnp.arange(L, dtype=jnp.int32)
    means = []
    for g, w in enumerate(POOL_WINDOWS):
        sl = slice(g * POOL_GROUP_W, (g + 1) * POOL_GROUP_W)
        tot = cs[:, POOL_HIST + 1:, sl] - cs[:, POOL_HIST + 1 - w:POOL_HIST + 1 - w + L, sl]
        cnt = jnp.minimum(w, pos + 1).astype(jnp.float32)[None, :, None]
        means.append(tot / cnt)
    d = (jnp.concatenate(means, axis=-1) - u.astype(jnp.float32)).astype(u.dtype)
    d = d.reshape(b, L, N_POOL_GROUPS, POOL_GROUP_W)
    y = jnp.einsum('blgc,gcd->blgd', d, pool_w_l).reshape(b, L, POOL_W)
    return y * pool_scale_l


def token_mixer(h, pos0, k_hist, v_hist, pool_hist, w_in_l, q_gain, k_gain, sinks_l,
                pool_w_l, pool_scale_l, w_attn_br_l, w_pool_br_l, w_out_l):
    b, L, _ = h.shape
    pos = pos0 + jnp.arange(L, dtype=jnp.int32)
    q, k, v, u, ga, gb = jnp.split(h @ w_in_l, SPLIT_AT, axis=-1)
    q = apply_partial_rope(rms_norm(q.reshape(b, L, N_HEADS, HEAD_DIM), q_gain), pos)
    k = apply_partial_rope(rms_norm(k.reshape(b, L, N_KV_HEADS, HEAD_DIM), k_gain), pos)
    v = v.reshape(b, L, N_KV_HEADS, HEAD_DIM)
    if k_hist is None:
        o = banded_window_attention(q, k, v, sinks_l)
        new_k, new_v = k[:, -WINDOW:], v[:, -WINDOW:]
        hist = jnp.zeros((b, POOL_HIST, POOL_W), u.dtype)
    else:
        k_all = jnp.concatenate([k_hist, k], axis=1)
        v_all = jnp.concatenate([v_hist, v], axis=1)
        qc = q.reshape(b, 1, L, N_KV_HEADS, GROUP, HEAD_DIM)
        valid = jnp.ones((1, k_all.shape[1]), dtype=bool)
        o = sink_attention(qc, k_all[:, None], v_all[:, None], valid, sinks_l).reshape(b, L, ATTN_W)
        new_k, new_v = k_all[:, -WINDOW:], v_all[:, -WINDOW:]
        hist = pool_hist
    pooled = pool_mixer(u, hist, pos0, pool_w_l, pool_scale_l)
    new_pool = jnp.concatenate([hist, u], axis=1)[:, -POOL_HIST:]
    a = o @ w_attn_br_l
    c = pooled @ w_pool_br_l
    out = (jax.nn.sigmoid(ga) * a + jax.nn.sigmoid(gb) * c) @ w_out_l
    return out, new_k, new_v, new_pool


def swiglu(h, w_gate_up, w_down):
    g, u = jnp.split(h @ w_gate_up, 2, axis=-1)
    return (jax.nn.silu(g) * u) @ w_down


def moe_ffn(h, router, w_gate_up, w_down):
    logits = (h @ router).astype(jnp.float32)
    top_v, top_i = lax.top_k(logits, TOP_K)
    wts = jax.nn.softmax(top_v, axis=-1)
    combine = jnp.sum(jax.nn.one_hot(top_i, N_EXPERTS, dtype=jnp.float32) * wts[..., None], axis=-2)
    combine = combine.astype(h.dtype)
    out = jnp.zeros_like(h)
    for e in range(N_EXPERTS):
        out = out + combine[..., e:e + 1] * swiglu(h, w_gate_up[e], w_down[e])
    return out


def setup_inputs(seed: int = 0) -> dict:
    key = jax.random.key(seed)
    ks = jax.random.split(key, 32)
    n = lambda i, shape: jax.random.normal(ks[i], shape, jnp.float32)
    return {
        "x_prompt": n(0, (BATCH, SEQ, D_MODEL)),
        "x_sample": n(1, (DEC_BATCH, DEC_SEQ, D_MODEL)),
        "cache_k": n(2, (DEPTH, DEC_BATCH, WINDOW, N_KV_HEADS, HEAD_DIM)),
        "cache_v": n(3, (DEPTH, DEC_BATCH, WINDOW, N_KV_HEADS, HEAD_DIM)),
        "state_pool": n(4, (DEPTH, DEC_BATCH, POOL_HIST, POOL_W)),
        "p_prompt": n(5, (DEPTH, BATCH, SEQ, PLE_DIM)),
        "p_sample": n(6, (DEPTH, DEC_BATCH, DEC_SEQ, PLE_DIM)),
        "norm_mix": 1.0 + 0.1 * n(7, (DEPTH, D_MODEL)),
        "w_in": n(8, (DEPTH, D_MODEL, IN_W)) * D_MODEL ** -0.5,
        "q_norm": 1.0 + 0.1 * n(9, (DEPTH, HEAD_DIM)),
        "k_norm": 1.0 + 0.1 * n(10, (DEPTH, HEAD_DIM)),
        "sinks": 0.5 * n(11, (DEPTH, N_HEADS)),
        "pool_w": n(12, (DEPTH, N_POOL_GROUPS, POOL_GROUP_W, POOL_GROUP_W)) * POOL_GROUP_W ** -0.5,
        "pool_scale": 1.0 + 0.1 * n(13, (DEPTH, POOL_W)),
        "w_attn_br": n(14, (DEPTH, ATTN_W, D_MODEL)) * ATTN_W ** -0.5,
        "w_pool_br": n(15, (DEPTH, POOL_W, D_MODEL)) * POOL_W ** -0.5,
        "w_out": n(16, (DEPTH, D_MODEL, D_MODEL)) * D_MODEL ** -0.5,
        "norm_ffn": 1.0 + 0.1 * n(17, (DEPTH, D_MODEL)),
        "ffn_w_gate_up": n(18, (N_DENSE, D_MODEL, 2 * D_FF)) * D_MODEL ** -0.5,
        "ffn_w_down": n(19, (N_DENSE, D_FF, D_MODEL)) * D_FF ** -0.5,
        "moe_router": n(20, (N_MOE, D_MODEL, N_EXPERTS)) * D_MODEL ** -0.5,
        "moe_w_gate_up": n(21, (N_MOE, N_EXPERTS, D_MODEL, 2 * D_FF_EXPERT)) * D_MODEL ** -0.5,
        "moe_w_down": n(22, (N_MOE, N_EXPERTS, D_FF_EXPERT, D_MODEL)) * D_FF_EXPERT ** -0.5,
        "norm_ple": 1.0 + 0.1 * n(23, (DEPTH, D_MODEL)),
        "w_ple": n(24, (DEPTH, PLE_DIM, D_MODEL)) * PLE_DIM ** -0.5,
        "w_ple_gate": n(25, (DEPTH, D_MODEL, D_MODEL)) * D_MODEL ** -0.5,
    }


def reference(x_prompt, x_sample, cache_k, cache_v, state_pool, p_prompt, p_sample,
              norm_mix, w_in, q_norm, k_norm, sinks, pool_w, pool_scale, w_attn_br, w_pool_br,
              w_out, norm_ffn, ffn_w_gate_up, ffn_w_down, moe_router, moe_w_gate_up, moe_w_down,
              norm_ple, w_ple, w_ple_gate):
    def run_layer(l, x, p_l, pos0, k_hist, v_hist, pool_hist):
        h = rms_norm(x, norm_mix[l])
        mix, nk, nv, npool = token_mixer(h, pos0, k_hist, v_hist, pool_hist, w_in[l], q_norm[l],
                                         k_norm[l], sinks[l], pool_w[l], pool_scale[l],
                                         w_attn_br[l], w_pool_br[l], w_out[l])
        x = x + mix
        h2 = rms_norm(x, norm_ffn[l])
        if l % 2 == 0:
            x = x + swiglu(h2, ffn_w_gate_up[l // 2], ffn_w_down[l // 2])
        else:
            x = x + moe_ffn(h2, moe_router[l // 2], moe_w_gate_up[l // 2], moe_w_down[l // 2])
        gate = jax.nn.sigmoid(rms_norm(x, norm_ple[l]) @ w_ple_gate[l])
        x = x + (p_l @ w_ple[l]) * gate
        return x, nk, nv, npool

    xp, xs = x_prompt, x_sample
    kp_l, vp_l, pp_l, ks_l, vs_l, ps_l = [], [], [], [], [], []
    for l in range(DEPTH):
        xp, nk, nv, npool = run_layer(l, xp, p_prompt[l], 0, None, None, None)
        kp_l.append(nk); vp_l.append(nv); pp_l.append(npool)
        xs, nk, nv, npool = run_layer(l, xs, p_sample[l], PAST_LEN, cache_k[l], cache_v[l],
                                      state_pool[l])
        ks_l.append(nk); vs_l.append(nv); ps_l.append(npool)
    new_k_prompt = jnp.stack(kp_l)
    new_v_prompt = jnp.stack(vp_l)
    new_pool_prompt = jnp.stack(pp_l)
    new_k_sample = jnp.stack(ks_l)
    new_v_sample = jnp.stack(vs_l)
    new_pool_sample = jnp.stack(ps_l)
    return (xp, xs, new_k_prompt, new_v_prompt, new_pool_prompt, new_k_sample, new_v_sample, new_pool_sample)
```

```python
import functools

import jax
import jax.numpy as jnp
from jax import lax
from jax.experimental import pallas as pl
from jax.experimental.pallas import tpu as pltpu

F32 = jnp.float32
BF16 = jnp.bfloat16
I32 = jnp.int32

D_MODEL = 2048
BATCH = 2
SEQ = 4096
DEPTH = 2
DEC_BATCH = 16
DEC_SEQ = 64
PAST_LEN = 2048
CHUNK = 64
N_HEADS = 16
N_KV_HEADS = 4
HEAD_DIM = 64
GROUP = N_HEADS // N_KV_HEADS
WINDOW = 128
ROT_DIM = HEAD_DIM // 4
ROPE_THETA = 500000.0
ATTN_SCALE = HEAD_DIM ** -0.5
POOL_WINDOWS = (2, 4, 8, 16)
POOL_W = 1024
POOL_GROUP_W = 256
POOL_HIST = 15
ATTN_W = N_HEADS * HEAD_DIM
KV_W = N_KV_HEADS * HEAD_DIM
IN_W = ATTN_W + 2 * KV_W + POOL_W + 2 * D_MODEL
D_FF = 5632
N_EXPERTS = 8
PLE_DIM = 256
RMS_EPS = 1e-6
NEG_INF = -1e30

TP = BATCH * SEQ
TS = DEC_BATCH * DEC_SEQ
T = TP + TS

LANES = 128
HALO = 16
MOE_TILE = 256
MOE_TILES = (2 * T + N_EXPERTS * (MOE_TILE - 1)) // MOE_TILE
MOE_ROWS = MOE_TILES * MOE_TILE
FF_BLK = 512
N_FF_BLK = D_FF // FF_BLK


def _cp(n_axes, vmem_mb):
    return pltpu.CompilerParams(dimension_semantics=("arbitrary",) * n_axes,
                                vmem_limit_bytes=vmem_mb * 2 ** 20)


def _dot(a, b):
    return jnp.dot(a, b, preferred_element_type=F32)


def _rms(x, g):
    ms = jnp.mean(x * x, axis=-1, keepdims=True)
    return (x * lax.rsqrt(ms + RMS_EPS)) * g


def _rmsnorm_body(x_ref, g_ref, o_ref):
    o_ref[...] = _rms(x_ref[...], g_ref[...]).astype(o_ref.dtype)


def rmsnorm_bf16(x, g):
    tm = 512
    return pl.pallas_call(
        _rmsnorm_body,
        grid=(T // tm,),
        in_specs=[pl.BlockSpec((tm, D_MODEL), lambda i: (i, 0)),
                  pl.BlockSpec((1, D_MODEL), lambda i: (0, 0))],
        out_specs=pl.BlockSpec((tm, D_MODEL), lambda i: (i, 0)),
        out_shape=jax.ShapeDtypeStruct((T, D_MODEL), BF16),
        compiler_params=_cp(1, 32),
        name="rmsnorm",
    )(x, g.reshape(1, D_MODEL))


def _proj_plain_body(a_ref, w_ref, o_ref, wbf):
    @pl.when(pl.program_id(1) == 0)
    def _():
        wbf[...] = w_ref[...].astype(BF16)

    o_ref[...] = _dot(a_ref[...], wbf[...]).astype(o_ref.dtype)


def _proj_qk_body(a_ref, w_ref, gain_ref, cos_ref, sin_ref, o_ref, wbf, seg):
    @pl.when(pl.program_id(1) == 0)
    def _():
        wbf[...] = w_ref[...].astype(BF16)
        r = lax.shift_right_logical(lax.broadcasted_iota(I32, seg.shape, 0), 6)
        c = lax.shift_right_logical(lax.broadcasted_iota(I32, seg.shape, 1), 6)
        seg[...] = jnp.where(r == c, 1.0, 0.0).astype(BF16)

    acc = _dot(a_ref[...], wbf[...])
    tm, tn = acc.shape
    sq = acc * acc
    hi = sq.astype(BF16)
    lo = (sq - hi.astype(F32)).astype(BF16)
    gain = gain_ref[...]
    cos = cos_ref[...]
    sin = sin_ref[...]
    first = (lax.broadcasted_iota(I32, (tm, LANES), 1) & (HEAD_DIM - 1)) < ROT_DIM // 2
    for cc in range(tn // 256):
        sl = slice(cc * 256, (cc + 1) * 256)
        ssum = _dot(hi[:, sl], seg[...]) + _dot(lo[:, sl], seg[...])
        y = (acc[:, sl] * lax.rsqrt(ssum * (1.0 / HEAD_DIM) + RMS_EPS)) * gain[:, sl]
        for hh in range(256 // LANES):
            yc = y[:, hh * LANES:(hh + 1) * LANES]
            up = pltpu.roll(yc, LANES - ROT_DIM // 2, axis=1)
            dn = pltpu.roll(yc, ROT_DIM // 2, axis=1)
            rot = yc * cos + jnp.where(first, up, dn) * sin
            lo_c = cc * 256 + hh * LANES
            o_ref[:, lo_c:lo_c + LANES] = rot.astype(o_ref.dtype)


def project(h, w_in, layer, col_blk, ncols, tn, tm, out_dtype, qk=None):
    nj = ncols // tn
    in_specs = [pl.BlockSpec((tm, D_MODEL), lambda j, t: (t, 0)),
                pl.BlockSpec((None, D_MODEL, tn), lambda j, t: (layer, 0, col_blk + j))]
    scratch = [pltpu.VMEM((D_MODEL, tn), BF16)]
    args = [h, w_in]
    if qk is None:
        body = _proj_plain_body
    else:
        gain, cos_t, sin_t = qk
        body = _proj_qk_body
        in_specs += [pl.BlockSpec((1, tn), lambda j, t: (0, j)),
                     pl.BlockSpec((tm, LANES), lambda j, t: (t, 0)),
                     pl.BlockSpec((tm, LANES), lambda j, t: (t, 0))]
        scratch.append(pltpu.VMEM((256, 256), BF16))
        args += [gain, cos_t, sin_t]
    return pl.pallas_call(
        body,
        grid=(nj, T // tm),
        in_specs=in_specs,
        out_specs=pl.BlockSpec((tm, tn), lambda j, t: (t, j)),
        out_shape=jax.ShapeDtypeStruct((T, ncols), out_dtype),
        scratch_shapes=scratch,
        compiler_params=_cp(2, 48),
        name="proj_qk" if qk is not None else "proj",
    )(*args)


def rope_tables():
    pos = jnp.concatenate([jnp.tile(jnp.arange(SEQ, dtype=I32), BATCH),
                           jnp.tile(PAST_LEN + jnp.arange(DEC_SEQ, dtype=I32), DEC_BATCH)])
    inv = ROPE_THETA ** (-jnp.arange(0, ROT_DIM, 2, dtype=F32) / ROT_DIM)
    ang = pos.astype(F32)[:, None] * inv[None, :]
    cos8, sin8 = jnp.cos(ang), jnp.sin(ang)
    rest = HEAD_DIM - ROT_DIM
    c64 = jnp.concatenate([cos8, cos8, jnp.ones((T, rest), F32)], axis=1)
    s64 = jnp.concatenate([-sin8, sin8, jnp.zeros((T, rest), F32)], axis=1)
    return jnp.tile(c64, (1, LANES // HEAD_DIM)), jnp.tile(s64, (1, LANES // HEAD_DIM))


def _lane_shift_variants(a):
    a0, a1 = a[:, :LANES], a[:, LANES:]
    r0 = pltpu.roll(a0, HEAD_DIM, axis=1)
    r1 = pltpu.roll(a1, HEAD_DIM, axis=1)
    low = lax.broadcasted_iota(I32, a0.shape, 1) < HEAD_DIM
    rot = jnp.concatenate([jnp.where(low, r1, r0), jnp.where(low, r0, r1)], axis=1)
    swap = lambda z: jnp.concatenate([z[:, LANES:], z[:, :LANES]], axis=1)
    return [a.astype(BF16), rot.astype(BF16), swap(a).astype(BF16), swap(rot).astype(BF16)]


def _attn_body(sinks_ref, q_ref, kh_ref, km_ref, vh_ref, vm_ref, o_ref, *, cb, mask_first):
    i = pl.program_id(1)
    kvar = _lane_shift_variants(jnp.concatenate([kh_ref[...], km_ref[...]], axis=0))
    vvar = _lane_shift_variants(jnp.concatenate([vh_ref[...], vm_ref[...]], axis=0))
    rows = kvar[0].shape[0]
    blk = lax.shift_right_logical(lax.broadcasted_iota(I32, (rows, KV_W), 1), 6)
    zero = jnp.zeros((rows, KV_W), BF16)
    nkeys = WINDOW + CHUNK
    kidx = lax.broadcasted_iota(I32, (CHUNK, nkeys), 1)
    for x in range(N_KV_HEADS):
        kexp = [jnp.where(blk == j, kvar[(j - x) % GROUP], zero) for j in range(GROUP)]
        vexp = [jnp.where(blk == j, vvar[(j - x) % GROUP], zero) for j in range(GROUP)]
        for c in range(cb):
            qg = q_ref[c * CHUNK:(c + 1) * CHUNK, x * 256:(x + 1) * 256]
            og = jnp.zeros((CHUNK, 256), F32)
            for j in range(GROUP):
                kk = kexp[j][c * CHUNK:c * CHUNK + nkeys]
                s = lax.dot_general(qg, kk, (((1,), (1,)), ((), ())),
                                    preferred_element_type=F32) * ATTN_SCALE
                if mask_first and c * CHUNK < WINDOW:
                    thr = jnp.where(i == 0, WINDOW - c * CHUNK, 0)
                    s = jnp.where(kidx >= thr, s, NEG_INF)
                sink = sinks_ref[x * GROUP + j]
                m = jnp.maximum(jnp.max(s, axis=1, keepdims=True), sink)
                p = jnp.exp(s - m)
                den = jnp.sum(p, axis=1, keepdims=True) + jnp.exp(sink - m)
                pn = (p / den).astype(BF16)
                og = og + _dot(pn, vexp[j][c * CHUNK:c * CHUNK + nkeys])
            o_ref[c * CHUNK:(c + 1) * CHUNK, x * 256:(x + 1) * 256] = og.astype(o_ref.dtype)


def attention_prompt(sinks, q, k, v):
    cb = 4
    rows = cb * CHUNK
    steps = SEQ // rows
    hb = rows // WINDOW
    main = lambda b, i, s: (b * steps + i, 0)
    halo = lambda b, i, s: (jnp.maximum(b * steps * hb + i * hb - 1, b * steps * hb), 0)
    return pl.pallas_call(
        functools.partial(_attn_body, cb=cb, mask_first=True),
        grid_spec=pltpu.PrefetchScalarGridSpec(
            num_scalar_prefetch=1,
            grid=(BATCH, steps),
            in_specs=[pl.BlockSpec((rows, ATTN_W), main),
                      pl.BlockSpec((WINDOW, KV_W), halo),
                      pl.BlockSpec((rows, KV_W), main),
                      pl.BlockSpec((WINDOW, KV_W), halo),
                      pl.BlockSpec((rows, KV_W), main)],
            out_specs=pl.BlockSpec((rows, ATTN_W), main)),
        out_shape=jax.ShapeDtypeStruct((TP, ATTN_W), BF16),
        compiler_params=_cp(2, 32),
        name="attn_prompt",
    )(sinks, q, k, k, v, v)


def attention_sample(sinks, q, k, v, cache_k, cache_v):
    first = TP // DEC_SEQ
    main = lambda b, i, s: (first + b, 0)
    halo = lambda b, i, s: (b, 0)
    return pl.pallas_call(
        functools.partial(_attn_body, cb=1, mask_first=False),
        grid_spec=pltpu.PrefetchScalarGridSpec(
            num_scalar_prefetch=1,
            grid=(DEC_BATCH, 1),
            in_specs=[pl.BlockSpec((DEC_SEQ, ATTN_W), main),
                      pl.BlockSpec((WINDOW, KV_W), halo),
                      pl.BlockSpec((DEC_SEQ, KV_W), main),
                      pl.BlockSpec((WINDOW, KV_W), halo),
                      pl.BlockSpec((DEC_SEQ, KV_W), main)],
            out_specs=pl.BlockSpec((DEC_SEQ, ATTN_W), halo)),
        out_shape=jax.ShapeDtypeStruct((TS, ATTN_W), BF16),
        compiler_params=_cp(2, 32),
        name="attn_sample",
    )(sinks, q, cache_k, k, cache_v, v)


def _pool_body(u_ref, h_ref, pw_ref, sc_ref, y_ref, *, zero_first, full_count):
    i = pl.program_id(1)
    u = u_ref[...]
    tl = u.shape[0]
    halo = h_ref[...]
    if zero_first:
        halo = jnp.where(i == 0, 0.0, halo)
    ext = jnp.concatenate([halo, u], axis=0)
    pos = i * tl + lax.broadcasted_iota(I32, (tl, 1), 0)
    for g, w in enumerate(POOL_WINDOWS):
        sl = slice(g * POOL_GROUP_W, (g + 1) * POOL_GROUP_W)
        s = ext[:, sl]
        sh = 1
        while sh < w:
            s = s + pltpu.roll(s, sh, axis=0)
            sh *= 2
        tot = s[HALO:]
        cnt = float(w) if full_count else jnp.minimum(w, pos + 1).astype(F32)
        d = (tot / cnt - u[:, sl]).astype(BF16)
        yg = _dot(d, pw_ref[g].astype(BF16)) * sc_ref[:, sl]
        y_ref[:, sl] = yg.astype(y_ref.dtype)


def pool_prompt(u, pool_w, pool_scale, layer):
    tl = 512
    steps = SEQ // tl
    hpb = tl // HALO
    main = lambda b, i: (b * steps + i, 0)
    halo = lambda b, i: (jnp.maximum((b * steps + i) * hpb - 1, b * steps * hpb), 0)
    return pl.pallas_call(
        functools.partial(_pool_body, zero_first=True, full_count=False),
        grid=(BATCH, steps),
        in_specs=[pl.BlockSpec((tl, POOL_W), main),
                  pl.BlockSpec((HALO, POOL_W), halo),
                  pl.BlockSpec((None, 4, POOL_GROUP_W, POOL_GROUP_W), lambda b, i: (layer, 0, 0, 0)),
                  pl.BlockSpec((1, POOL_W), lambda b, i: (0, 0))],
        out_specs=pl.BlockSpec((tl, POOL_W), main),
        out_shape=jax.ShapeDtypeStruct((TP, POOL_W), BF16),
        compiler_params=_cp(2, 32),
        name="pool_prompt",
    )(u, u, pool_w, pool_scale)


def pool_sample(u, hist, pool_w, pool_scale, layer):
    first = TP // DEC_SEQ
    return pl.pallas_call(
        functools.partial(_pool_body, zero_first=False, full_count=True),
        grid=(DEC_BATCH, 1),
        in_specs=[pl.BlockSpec((DEC_SEQ, POOL_W), lambda b, i: (first + b, 0)),
                  pl.BlockSpec((HALO, POOL_W), lambda b, i: (b, 0)),
                  pl.BlockSpec((None, 4, POOL_GROUP_W, POOL_GROUP_W), lambda b, i: (layer, 0, 0, 0)),
                  pl.BlockSpec((1, POOL_W), lambda b, i: (0, 0))],
        out_specs=pl.BlockSpec((DEC_SEQ, POOL_W), lambda b, i: (b, 0)),
        out_shape=jax.ShapeDtypeStruct((TS, POOL_W), BF16),
        compiler_params=_cp(2, 32),
        name="pool_sample",
    )(u, hist, pool_w, pool_scale)


def _branch_body(h_ref, op_ref, os_ref, yp_ref, ys_ref, wga_ref, wgb_ref, wa_ref, wp_ref, out_ref,
                 wga, wgb, wa, wp, *, prompt_blocks):
    t = pl.program_id(1)

    @pl.when(t == 0)
    def _():
        wga[...] = wga_ref[...].astype(BF16)
        wgb[...] = wgb_ref[...].astype(BF16)
        wa[...] = wa_ref[...].astype(BF16)
        wp[...] = wp_ref[...].astype(BF16)

    h = h_ref[...]
    is_prompt = t < prompt_blocks
    o = jnp.where(is_prompt, op_ref[...], os_ref[...])
    y = jnp.where(is_prompt, yp_ref[...], ys_ref[...])
    ga = jax.nn.sigmoid(_dot(h, wga[...]))
    gb = jax.nn.sigmoid(_dot(h, wgb[...]))
    a = _dot(o, wa[...])
    c = _dot(y, wp[...])
    out_ref[...] = (ga * a + gb * c).astype(out_ref.dtype)


def branch_mix(h, o_p, o_s, y_p, y_s, w_in, w_attn_br, w_pool_br, layer):
    tn, tm = 512, 512
    ga_blk = (ATTN_W + 2 * KV_W + POOL_W) // tn
    gb_blk = ga_blk + D_MODEL // tn
    pb = TP // tm
    prompt = lambda j, t: (jnp.minimum(t, pb - 1), 0)
    sample = lambda j, t: (jnp.maximum(t - pb, 0), 0)
    return pl.pallas_call(
        functools.partial(_branch_body, prompt_blocks=pb),
        grid=(D_MODEL // tn, T // tm),
        in_specs=[pl.BlockSpec((tm, D_MODEL), lambda j, t: (t, 0)),
                  pl.BlockSpec((tm, ATTN_W), prompt),
                  pl.BlockSpec((tm, ATTN_W), sample),
                  pl.BlockSpec((tm, POOL_W), prompt),
                  pl.BlockSpec((tm, POOL_W), sample),
                  pl.BlockSpec((None, D_MODEL, tn), lambda j, t: (layer, 0, ga_blk + j)),
                  pl.BlockSpec((None, D_MODEL, tn), lambda j, t: (layer, 0, gb_blk + j)),
                  pl.BlockSpec((None, ATTN_W, tn), lambda j, t: (layer, 0, j)),
                  pl.BlockSpec((None, POOL_W, tn), lambda j, t: (layer, 0, j))],
        out_specs=pl.BlockSpec((tm, tn), lambda j, t: (t, j)),
        out_shape=jax.ShapeDtypeStruct((T, D_MODEL), BF16),
        scratch_shapes=[pltpu.VMEM((D_MODEL, tn), BF16), pltpu.VMEM((D_MODEL, tn), BF16),
                        pltpu.VMEM((ATTN_W, tn), BF16), pltpu.VMEM((POOL_W, tn), BF16)],
        compiler_params=_cp(2, 56),
        name="branch_mix",
    )(h, o_p, o_s, y_p, y_s, w_in, w_in, w_attn_br, w_pool_br)


def _wout_body(m_ref, x_ref, w_ref, g_ref, *rest, router):
    if router:
        r_ref, x1_ref, h2_ref, lg_ref, wbf = rest
    else:
        x1_ref, h2_ref, wbf = rest

    @pl.when(pl.program_id(0) == 0)
    def _():
        wbf[...] = w_ref[...].astype(BF16)

    x1 = x_ref[...] + _dot(m_ref[...], wbf[...])
    x1_ref[...] = x1
    h2 = _rms(x1, g_ref[...])
    h2_ref[...] = h2.astype(h2_ref.dtype)
    if router:
        r = r_ref[...]
        r_hi = r.astype(BF16)
        r_lo = (r - r_hi.astype(F32)).astype(BF16)
        h_hi = h2.astype(BF16)
        h_lo = (h2 - h_hi.astype(F32)).astype(BF16)
        lg_ref[...] = _dot(h_hi, r_hi) + (_dot(h_lo, r_hi) + _dot(h_hi, r_lo))


def out_proj(mixed, x, w_out, g, layer, router_w=None):
    tm = 256
    router = router_w is not None
    row = lambda t: (t, 0)
    const2 = lambda t: (0, 0)
    in_specs = [pl.BlockSpec((tm, D_MODEL), row),
                pl.BlockSpec((tm, D_MODEL), row),
                pl.BlockSpec((None, D_MODEL, D_MODEL), lambda t: (layer, 0, 0),
                             pipeline_mode=pl.Buffered(1)),
                pl.BlockSpec((1, D_MODEL), const2)]
    out_specs = [pl.BlockSpec((tm, D_MODEL), row), pl.BlockSpec((tm, D_MODEL), row)]
    out_shape = [jax.ShapeDtypeStruct((T, D_MODEL), F32),
                 jax.ShapeDtypeStruct((T, D_MODEL), F32 if router else BF16)]
    args = [mixed, x, w_out, g.reshape(1, D_MODEL)]
    if router:
        in_specs.append(pl.BlockSpec((D_MODEL, LANES), const2))
        out_specs.append(pl.BlockSpec((tm, LANES), row))
        out_shape.append(jax.ShapeDtypeStruct((T, LANES), F32))
        args.append(router_w)
    return pl.pallas_call(
        functools.partial(_wout_body, router=router),
        grid=(T // tm,),
        in_specs=in_specs,
        out_specs=out_specs,
        out_shape=out_shape,
        scratch_shapes=[pltpu.VMEM((D_MODEL, D_MODEL), BF16)],
        compiler_params=_cp(1, 56),
        name="out_proj",
    )(*args)


def _ffn_up_body(h_ref, wg_ref, wu_ref, o_ref, wg, wu):
    @pl.when(pl.program_id(1) == 0)
    def _():
        wg[...] = wg_ref[...].astype(BF16)
        wu[...] = wu_ref[...].astype(BF16)

    h = h_ref[...]
    g = _dot(h, wg[...])
    o_ref[...] = (jax.nn.silu(g) * _dot(h, wu[...])).astype(o_ref.dtype)


def ffn_up(h2, w_gate_up, idx):
    tm = 1024
    return pl.pallas_call(
        _ffn_up_body,
        grid=(N_FF_BLK, T // tm),
        in_specs=[pl.BlockSpec((tm, D_MODEL), lambda j, t: (t, 0)),
                  pl.BlockSpec((None, D_MODEL, FF_BLK), lambda j, t: (idx, 0, j)),
                  pl.BlockSpec((None, D_MODEL, FF_BLK), lambda j, t: (idx, 0, N_FF_BLK + j))],
        out_specs=pl.BlockSpec((tm, FF_BLK), lambda j, t: (t, j)),
        out_shape=jax.ShapeDtypeStruct((T, D_FF), BF16),
        scratch_shapes=[pltpu.VMEM((D_MODEL, FF_BLK), BF16), pltpu.VMEM((D_MODEL, FF_BLK), BF16)],
        compiler_params=_cp(2, 48),
        name="ffn_up",
    )(h2, w_gate_up, w_gate_up)


def _ffn_down_body(a_ref, w_ref, x_ref, o_ref, wbf):
    @pl.when(pl.program_id(1) == 0)
    def _():
        wbf[...] = w_ref[...].astype(BF16)

    o_ref[...] = x_ref[...] + _dot(a_ref[...], wbf[...])


def ffn_down(act, w_down, idx, x1):
    tm, tn = 512, 512
    return pl.pallas_call(
        _ffn_down_body,
        grid=(D_MODEL // tn, T // tm),
        in_specs=[pl.BlockSpec((tm, D_FF), lambda j, t: (t, 0)),
                  pl.BlockSpec((None, D_FF, tn), lambda j, t: (idx, 0, j)),
                  pl.BlockSpec((tm, tn), lambda j, t: (t, j))],
        out_specs=pl.BlockSpec((tm, tn), lambda j, t: (t, j)),
        out_shape=jax.ShapeDtypeStruct((T, D_MODEL), F32),
        scratch_shapes=[pltpu.VMEM((D_FF, tn), BF16)],
        compiler_params=_cp(2, 56),
        name="ffn_down",
    )(act, w_down, x1)


def _ple_body(h_ref, p_ref, x_ref, wg_ref, wp_ref, o_ref, wg, wp):
    @pl.when(pl.program_id(0) == 0)
    def _():
        wg[...] = wg_ref[...].astype(BF16)
        wp[...] = wp_ref[...].astype(BF16)

    gate = jax.nn.sigmoid(_dot(h_ref[...], wg[...]))
    o_ref[...] = x_ref[...] + _dot(p_ref[...].astype(BF16), wp[...]) * gate


def ple(h3, p, x2, w_ple_gate, w_ple, layer):
    tm = 256
    row = lambda t: (t, 0)
    return pl.pallas_call(
        _ple_body,
        grid=(T // tm,),
        in_specs=[pl.BlockSpec((tm, D_MODEL), row),
                  pl.BlockSpec((tm, PLE_DIM), row),
                  pl.BlockSpec((tm, D_MODEL), row),
                  pl.BlockSpec((None, D_MODEL, D_MODEL), lambda t: (layer, 0, 0),
                               pipeline_mode=pl.Buffered(1)),
                  pl.BlockSpec((None, PLE_DIM, D_MODEL), lambda t: (layer, 0, 0),
                               pipeline_mode=pl.Buffered(1))],
        out_specs=pl.BlockSpec((tm, D_MODEL), row),
        out_shape=jax.ShapeDtypeStruct((T, D_MODEL), F32),
        scratch_shapes=[pltpu.VMEM((D_MODEL, D_MODEL), BF16), pltpu.VMEM((PLE_DIM, D_MODEL), BF16)],
        compiler_params=_cp(1, 56),
        name="ple",
    )(h3, p, x2, w_ple_gate, w_ple)


def _router_body(lg_ref, meta_ref, cnt_ref, carry):
    @pl.when(pl.program_id(0) == 0)
    def _():
        carry[...] = jnp.zeros_like(carry)

    lg = lg_ref[...]
    tm = lg.shape[0]
    lane = lax.broadcasted_iota(I32, (tm, LANES), 1)
    lane_f = lane.astype(F32)
    big = float(LANES)
    l1 = jnp.where(lane < N_EXPERTS, lg, -jnp.inf)
    m1 = jnp.max(l1, axis=1, keepdims=True)
    i1 = jnp.min(jnp.where(l1 == m1, lane_f, big), axis=1, keepdims=True)
    oh1 = lane_f == i1
    l2 = jnp.where(oh1, -jnp.inf, l1)
    m2 = jnp.max(l2, axis=1, keepdims=True)
    i2 = jnp.min(jnp.where(l2 == m2, lane_f, big), axis=1, keepdims=True)
    oh2 = lane_f == i2
    e = jnp.exp(m2 - m1)
    w1 = 1.0 / (1.0 + e)
    w2 = e / (1.0 + e)
    oh = jnp.where(oh1 | oh2, 1.0, 0.0)
    r = lax.broadcasted_iota(I32, (tm, tm), 0)
    c = lax.broadcasted_iota(I32, (tm, tm), 1)
    tri = jnp.where(c < r, 1.0, 0.0).astype(BF16)
    before = _dot(tri, oh.astype(BF16)) + carry[0:1, :]
    r1 = jnp.sum(jnp.where(oh1, before, 0.0), axis=1, keepdims=True)
    r2 = jnp.sum(jnp.where(oh2, before, 0.0), axis=1, keepdims=True)
    meta = jnp.where(lane == 0, i1, jnp.where(lane == 1, i2, jnp.where(lane == 2, w1,
           jnp.where(lane == 3, w2, jnp.where(lane == 4, r1, jnp.where(lane == 5, r2, 0.0))))))
    meta_ref[...] = meta
    total = carry[0:1, :] + jnp.sum(oh, axis=0, keepdims=True)
    carry[...] = jnp.broadcast_to(total, carry.shape)
    cnt_ref[...] = jnp.broadcast_to(total, cnt_ref.shape)


def route(logits):
    tm = 512
    return pl.pallas_call(
        _router_body,
        grid=(T // tm,),
        in_specs=[pl.BlockSpec((tm, LANES), lambda t: (t, 0))],
        out_specs=[pl.BlockSpec((tm, LANES), lambda t: (t, 0)),
                   pl.BlockSpec((8, LANES), lambda t: (0, 0))],
        out_shape=[jax.ShapeDtypeStruct((T, LANES), F32), jax.ShapeDtypeStruct((8, LANES), F32)],
        scratch_shapes=[pltpu.VMEM((8, LANES), F32)],
        compiler_params=_cp(1, 32),
        name="moe_route",
    )(logits)


def moe_plan(meta, cnt_rows):
    i1 = meta[:, 0].astype(I32)
    i2 = meta[:, 1].astype(I32)
    r1 = meta[:, 4].astype(I32)
    r2 = meta[:, 5].astype(I32)
    cnt = cnt_rows[0, :N_EXPERTS].astype(I32)
    tiles_e = (cnt + MOE_TILE - 1) // MOE_TILE
    cum_tiles = jnp.cumsum(tiles_e)
    pad_off = (cum_tiles - tiles_e) * MOE_TILE
    n_tiles = cum_tiles[-1:]
    pos1 = pad_off[i1] + r1
    pos2 = pad_off[i2] + r2
    e_ids = jnp.arange(N_EXPERTS, dtype=I32)
    last_e = jnp.max(jnp.where(cnt > 0, e_ids, 0))
    tile_ids = jnp.arange(MOE_TILES, dtype=I32)
    tile_e = jnp.minimum(jnp.sum(tile_ids[:, None] >= cum_tiles[None, :], axis=1).astype(I32), last_e)
    flat_e = jnp.stack([i1, i2], axis=1).reshape(-1)
    tok_sorted = (jnp.argsort(flat_e, stable=True) // 2).astype(I32)
    cstart = jnp.cumsum(cnt) - cnt
    rows = jnp.arange(MOE_ROWS, dtype=I32)
    e_row = tile_e[rows // MOE_TILE]
    local = jnp.clip(rows - pad_off[e_row], 0, jnp.maximum(cnt[e_row] - 1, 0))
    row_tok = tok_sorted[jnp.clip(cstart[e_row] + local, 0, 2 * T - 1)]
    return pos1, pos2, tile_e, n_tiles, row_tok


def _row_copy(src_hbm, dst_vmem, sem, src_row, dst_row):
    return pltpu.make_async_copy(src_hbm.at[pl.ds(src_row, 1)], dst_vmem.at[pl.ds(dst_row, 1)], sem)


def _gather_rows(idx_ref, src_hbm, dst_vmem, sem, n):
    def start(r, carry):
        _row_copy(src_hbm, dst_vmem, sem, idx_ref[0, 0, r], r).start()
        return carry

    def wait(r, carry):
        _row_copy(src_hbm, dst_vmem, sem, 0, r).wait()
        return carry

    lax.fori_loop(0, n, start, 0, unroll=8)
    lax.fori_loop(0, n, wait, 0, unroll=8)


def _zero_dead_tile(nt_ref, r, o_ref):
    @pl.when(r >= nt_ref[0])
    def _():
        o_ref[...] = jnp.zeros_like(o_ref)


def _dispatch_body(nt_ref, tok_ref, h_hbm, o_ref, buf, sem):
    r = pl.program_id(0)

    @pl.when(r < nt_ref[0])
    def _():
        _gather_rows(tok_ref, h_hbm, buf, sem, MOE_TILE)
        o_ref[...] = buf[...].astype(o_ref.dtype)

    _zero_dead_tile(nt_ref, r, o_ref)


def moe_dispatch(n_tiles, row_tok, h2f):
    return pl.pallas_call(
        _dispatch_body,
        grid_spec=pltpu.PrefetchScalarGridSpec(
            num_scalar_prefetch=1,
            grid=(MOE_TILES,),
            in_specs=[pl.BlockSpec((1, 1, MOE_TILE), lambda r, nt: (r, 0, 0), memory_space=pltpu.SMEM),
                      pl.BlockSpec(memory_space=pl.ANY)],
            out_specs=pl.BlockSpec((MOE_TILE, D_MODEL), lambda r, nt: (r, 0)),
            scratch_shapes=[pltpu.VMEM((MOE_TILE, D_MODEL), F32), pltpu.SemaphoreType.DMA(())]),
        out_shape=jax.ShapeDtypeStruct((MOE_ROWS, D_MODEL), BF16),
        compiler_params=_cp(1, 32),
        name="moe_dispatch",
    )(n_tiles, row_tok.reshape(MOE_TILES, 1, MOE_TILE), h2f)


def _expert_changed(te_ref, r):
    return jnp.logical_or(r == 0, te_ref[r] != te_ref[jnp.maximum(r - 1, 0)])


def _moe_up_body(te_ref, nt_ref, a_ref, wg_ref, wu_ref, o_ref, wg, wu):
    r = pl.program_id(1)

    @pl.when(r < nt_ref[0])
    def _():
        @pl.when(_expert_changed(te_ref, r))
        def _():
            wg[...] = wg_ref[...].astype(BF16)
            wu[...] = wu_ref[...].astype(BF16)

        a = a_ref[...]
        g = _dot(a, wg[...])
        o_ref[...] = (jax.nn.silu(g) * _dot(a, wu[...])).astype(o_ref.dtype)

    _zero_dead_tile(nt_ref, r, o_ref)


def moe_up(tile_e, n_tiles, xs, w_gate_up, idx):
    live = lambda j, r, te, nt: (jnp.minimum(r, nt[0] - 1), 0)
    return pl.pallas_call(
        _moe_up_body,
        grid_spec=pltpu.PrefetchScalarGridSpec(
            num_scalar_prefetch=2,
            grid=(N_FF_BLK, MOE_TILES),
            in_specs=[pl.BlockSpec((MOE_TILE, D_MODEL), live),
                      pl.BlockSpec((None, None, D_MODEL, FF_BLK),
                                   lambda j, r, te, nt: (idx, te[r], 0, j)),
                      pl.BlockSpec((None, None, D_MODEL, FF_BLK),
                                   lambda j, r, te, nt: (idx, te[r], 0, N_FF_BLK + j))],
            out_specs=pl.BlockSpec((MOE_TILE, FF_BLK), lambda j, r, te, nt: (r, j)),
            scratch_shapes=[pltpu.VMEM((D_MODEL, FF_BLK), BF16), pltpu.VMEM((D_MODEL, FF_BLK), BF16)]),
        out_shape=jax.ShapeDtypeStruct((MOE_ROWS, D_FF), BF16),
        compiler_params=_cp(2, 48),
        name="moe_up",
    )(tile_e, n_tiles, xs, w_gate_up, w_gate_up)


def _moe_down_body(te_ref, nt_ref, a_ref, w_ref, o_ref, wbf):
    r = pl.program_id(1)

    @pl.when(r < nt_ref[0])
    def _():
        @pl.when(_expert_changed(te_ref, r))
        def _():
            wbf[...] = w_ref[...].astype(BF16)

        o_ref[...] = _dot(a_ref[...], wbf[...])

    _zero_dead_tile(nt_ref, r, o_ref)


def moe_down(tile_e, n_tiles, act, w_down, idx):
    tn = 512
    return pl.pallas_call(
        _moe_down_body,
        grid_spec=pltpu.PrefetchScalarGridSpec(
            num_scalar_prefetch=2,
            grid=(D_MODEL // tn, MOE_TILES),
            in_specs=[pl.BlockSpec((MOE_TILE, D_FF), lambda j, r, te, nt: (jnp.minimum(r, nt[0] - 1), 0)),
                      pl.BlockSpec((None, None, D_FF, tn), lambda j, r, te, nt: (idx, te[r], 0, j))],
            out_specs=pl.BlockSpec((MOE_TILE, tn), lambda j, r, te, nt: (r, j)),
            scratch_shapes=[pltpu.VMEM((D_FF, tn), BF16)]),
        out_shape=jax.ShapeDtypeStruct((MOE_ROWS, D_MODEL), F32),
        compiler_params=_cp(2, 56),
        name="moe_down",
    )(tile_e, n_tiles, act, w_down)


def _combine_body(p1_ref, p2_ref, ys_hbm, meta_ref, x_ref, g_ref, x2_ref, h3_ref, b1, b2, sem1, sem2):
    n = x_ref.shape[0]
    _gather_rows(p1_ref, ys_hbm, b1, sem1, n)
    _gather_rows(p2_ref, ys_hbm, b2, sem2, n)
    meta = meta_ref[...]
    w1 = meta[:, 2:3]
    w2 = meta[:, 3:4]
    x2 = x_ref[...] + (w1 * b1[...] + w2 * b2[...])
    x2_ref[...] = x2
    h3_ref[...] = _rms(x2, g_ref[...]).astype(h3_ref.dtype)


def moe_combine(pos1, pos2, ys, meta, x1, g):
    tm = 256
    nblk = T // tm
    row = lambda t: (t, 0)
    smem = pl.BlockSpec((1, 1, tm), lambda t: (t, 0, 0), memory_space=pltpu.SMEM)
    return pl.pallas_call(
        _combine_body,
        grid=(nblk,),
        in_specs=[smem, smem,
                  pl.BlockSpec(memory_space=pl.ANY),
                  pl.BlockSpec((tm, LANES), row),
                  pl.BlockSpec((tm, D_MODEL), row),
                  pl.BlockSpec((1, D_MODEL), lambda t: (0, 0))],
        out_specs=[pl.BlockSpec((tm, D_MODEL), row), pl.BlockSpec((tm, D_MODEL), row)],
        out_shape=[jax.ShapeDtypeStruct((T, D_MODEL), F32), jax.ShapeDtypeStruct((T, D_MODEL), BF16)],
        scratch_shapes=[pltpu.VMEM((tm, D_MODEL), F32), pltpu.VMEM((tm, D_MODEL), F32),
                        pltpu.SemaphoreType.DMA(()), pltpu.SemaphoreType.DMA(())],
        compiler_params=_cp(1, 48),
        name="moe_combine",
    )(pos1.reshape(nblk, 1, tm), pos2.reshape(nblk, 1, tm), ys, meta, x1, g.reshape(1, D_MODEL))


def kernel(x_prompt, x_sample, cache_k, cache_v, state_pool, p_prompt, p_sample, norm_mix, w_in, q_norm, k_norm, sinks, pool_w, pool_scale, w_attn_br, w_pool_br, w_out, norm_ffn, ffn_w_gate_up, ffn_w_down, moe_router, moe_w_gate_up, moe_w_down, norm_ple, w_ple, w_ple_gate):
    x = jnp.concatenate([x_prompt.reshape(TP, D_MODEL), x_sample.reshape(TS, D_MODEL)], axis=0)
    cos_t, sin_t = rope_tables()
    outs = {k: [] for k in ("kp", "vp", "pp", "ks", "vs", "ps")}

    for l in range(DEPTH):
        h = rmsnorm_bf16(x, norm_mix[l])
        q_gain = jnp.tile(q_norm[l], ATTN_W // HEAD_DIM).reshape(1, ATTN_W)
        k_gain = jnp.tile(k_norm[l], KV_W // HEAD_DIM).reshape(1, KV_W)
        q = project(h, w_in, l, 0, ATTN_W, 512, 512, BF16, qk=(q_gain, cos_t, sin_t))
        k = project(h, w_in, l, ATTN_W // KV_W, KV_W, KV_W, 512, F32, qk=(k_gain, cos_t, sin_t))
        v = project(h, w_in, l, (ATTN_W + KV_W) // KV_W, KV_W, KV_W, 1024, F32)
        u = project(h, w_in, l, (ATTN_W + 2 * KV_W) // 512, POOL_W, 512, 1024, F32)

        ck = cache_k[l].reshape(DEC_BATCH * WINDOW, KV_W)
        cv = cache_v[l].reshape(DEC_BATCH * WINDOW, KV_W)
        o_p = attention_prompt(sinks[l], q, k, v)
        o_s = attention_sample(sinks[l], q, k, v, ck, cv)

        hist = jnp.pad(state_pool[l], ((0, 0), (HALO - POOL_HIST, 0), (0, 0))).reshape(DEC_BATCH * HALO, POOL_W)
        scale = pool_scale[l].reshape(1, POOL_W)
        y_p = pool_prompt(u, pool_w, scale, l)
        y_s = pool_sample(u, hist, pool_w, scale, l)

        mixed = branch_mix(h, o_p, o_s, y_p, y_s, w_in, w_attn_br, w_pool_br, l)

        if l % 2 == 0:
            x1, h2 = out_proj(mixed, x, w_out, norm_ffn[l], l)
            act = ffn_up(h2, ffn_w_gate_up, l // 2)
            x2 = ffn_down(act, ffn_w_down, l // 2, x1)
            h3 = rmsnorm_bf16(x2, norm_ple[l])
        else:
            router_w = jnp.pad(moe_router[l // 2], ((0, 0), (0, LANES - N_EXPERTS)))
            x1, h2f, logits = out_proj(mixed, x, w_out, norm_ffn[l], l, router_w=router_w)
            meta, cnt_rows = route(logits)
            pos1, pos2, tile_e, n_tiles, row_tok = moe_plan(meta, cnt_rows)
            xs = moe_dispatch(n_tiles, row_tok, h2f)
            act = moe_up(tile_e, n_tiles, xs, moe_w_gate_up, l // 2)
            ys = moe_down(tile_e, n_tiles, act, moe_w_down, l // 2)
            x2, h3 = moe_combine(pos1, pos2, ys, meta, x1, norm_ple[l])

        p = jnp.concatenate([p_prompt[l].reshape(TP, PLE_DIM), p_sample[l].reshape(TS, PLE_DIM)], axis=0)
        x = ple(h3, p, x2, w_ple_gate, w_ple, l)

        k4 = k.reshape(T, N_KV_HEADS, HEAD_DIM)
        v4 = v.reshape(T, N_KV_HEADS, HEAD_DIM)
        kp = k4[:TP].reshape(BATCH, SEQ, N_KV_HEADS, HEAD_DIM)
        vp = v4[:TP].reshape(BATCH, SEQ, N_KV_HEADS, HEAD_DIM)
        ksn = k4[TP:].reshape(DEC_BATCH, DEC_SEQ, N_KV_HEADS, HEAD_DIM)
        vsn = v4[TP:].reshape(DEC_BATCH, DEC_SEQ, N_KV_HEADS, HEAD_DIM)
        outs["kp"].append(kp[:, -WINDOW:])
        outs["vp"].append(vp[:, -WINDOW:])
        outs["ks"].append(jnp.concatenate([cache_k[l][:, DEC_SEQ:], ksn], axis=1))
        outs["vs"].append(jnp.concatenate([cache_v[l][:, DEC_SEQ:], vsn], axis=1))
        outs["pp"].append(u[:TP].reshape(BATCH, SEQ, POOL_W)[:, -POOL_HIST:])
        us = u[TP:].reshape(DEC_BATCH, DEC_SEQ, POOL_W)
        outs["ps"].append(jnp.concatenate([state_pool[l], us], axis=1)[:, -POOL_HIST:])

    y_prompt = x[:TP].reshape(BATCH, SEQ, D_MODEL)
    y_sample = x[TP:].reshape(DEC_BATCH, DEC_SEQ, D_MODEL)
    return (y_prompt, y_sample,
            jnp.stack(outs["kp"]), jnp.stack(outs["vp"]), jnp.stack(outs["pp"]),
            jnp.stack(outs["ks"]), jnp.stack(outs["vs"]), jnp.stack(outs["ps"]))
```

```python
import functools

import jax
import jax.numpy as jnp
from jax import lax
from jax.experimental import pallas as pl
from jax.experimental.pallas import tpu as pltpu

F32 = jnp.float32
BF16 = jnp.bfloat16
I32 = jnp.int32

D_MODEL = 2048
BATCH = 2
SEQ = 4096
DEPTH = 2
DEC_BATCH = 16
DEC_SEQ = 64
PAST_LEN = 2048
CHUNK = 64
N_HEADS = 16
N_KV_HEADS = 4
HEAD_DIM = 64
GROUP = N_HEADS // N_KV_HEADS
WINDOW = 128
ROT_DIM = HEAD_DIM // 4
ROPE_THETA = 500000.0
ATTN_SCALE = HEAD_DIM ** -0.5
POOL_WINDOWS = (2, 4, 8, 16)
POOL_W = 1024
POOL_GROUP_W = 256
POOL_HIST = 15
ATTN_W = N_HEADS * HEAD_DIM
KV_W = N_KV_HEADS * HEAD_DIM
IN_W = ATTN_W + 2 * KV_W + POOL_W + 2 * D_MODEL
D_FF = 5632
N_EXPERTS = 8
PLE_DIM = 256
RMS_EPS = 1e-6
NEG_INF = -1e30

TP = BATCH * SEQ
TS = DEC_BATCH * DEC_SEQ
T = TP + TS

LANES = 128
HALO = 16
KEY_WIN = 256
MOE_TILE = 512
MOE_PAD = N_EXPERTS * (MOE_TILE - 1)
MOE_TILES = (2 * T + MOE_PAD) // MOE_TILE
MOE_ROWS = MOE_TILES * MOE_TILE
FF_BLK = 512
N_FF_BLK = D_FF // FF_BLK


def _cp(n_axes, vmem_mb):
    return pltpu.CompilerParams(dimension_semantics=("arbitrary",) * n_axes,
                                vmem_limit_bytes=vmem_mb * 2 ** 20)


def _dot(a, b):
    return jnp.dot(a, b, preferred_element_type=F32)


def _rms(x, g):
    ms = jnp.mean(x * x, axis=-1, keepdims=True)
    return (x * lax.rsqrt(ms + RMS_EPS)) * g


def _row_specs(n_parts, tm, width, nd):
    if n_parts == 1:
        return [pl.BlockSpec((tm, width), lambda *g: (g[nd - 1], 0))]
    pb = TP // tm
    return [pl.BlockSpec((tm, width), lambda *g: (jnp.minimum(g[nd - 1], pb - 1), 0)),
            pl.BlockSpec((tm, width), lambda *g: (jnp.maximum(g[nd - 1] - pb, 0), 0))]


def _load_rows(refs, t):
    if len(refs) == 1:
        return refs[0][...]
    tm = refs[0].shape[0]
    return jnp.where(t < TP // tm, refs[0][...], refs[1][...])


def _rmsnorm_body(*refs):
    *x_refs, g_ref, o_ref = refs
    o_ref[...] = _rms(_load_rows(x_refs, pl.program_id(0)), g_ref[...]).astype(o_ref.dtype)


def rmsnorm_bf16(x_parts, g):
    tm = 512
    return pl.pallas_call(
        _rmsnorm_body,
        grid=(T // tm,),
        in_specs=_row_specs(len(x_parts), tm, D_MODEL, 1) + [pl.BlockSpec((1, D_MODEL), lambda i: (0, 0))],
        out_specs=pl.BlockSpec((tm, D_MODEL), lambda i: (i, 0)),
        out_shape=jax.ShapeDtypeStruct((T, D_MODEL), BF16),
        compiler_params=_cp(1, 32),
        name="rmsnorm",
    )(*x_parts, g.reshape(1, D_MODEL))


def _proj_plain_body(a_ref, w_ref, o_ref, wbf):
    @pl.when(pl.program_id(1) == 0)
    def _():
        wbf[...] = w_ref[...].astype(BF16)

    o_ref[...] = _dot(a_ref[...], wbf[...]).astype(o_ref.dtype)


def _proj_qk_body(a_ref, w_ref, gain_ref, cos_ref, sin_ref, o_ref, wbf, seg):
    @pl.when(pl.program_id(1) == 0)
    def _():
        wbf[...] = w_ref[...].astype(BF16)
        r = lax.shift_right_logical(lax.broadcasted_iota(I32, seg.shape, 0), 6)
        c = lax.shift_right_logical(lax.broadcasted_iota(I32, seg.shape, 1), 6)
        seg[...] = jnp.where(r == c, 1.0, 0.0).astype(BF16)

    acc = _dot(a_ref[...], wbf[...])
    tm, tn = acc.shape
    sq = acc * acc
    hi = sq.astype(BF16)
    lo = (sq - hi.astype(F32)).astype(BF16)
    gain = gain_ref[...]
    cos = cos_ref[...]
    sin = sin_ref[...]
    first = (lax.broadcasted_iota(I32, (tm, LANES), 1) & (HEAD_DIM - 1)) < ROT_DIM // 2
    for cc in range(tn // 256):
        sl = slice(cc * 256, (cc + 1) * 256)
        ssum = _dot(hi[:, sl], seg[...]) + _dot(lo[:, sl], seg[...])
        y = (acc[:, sl] * lax.rsqrt(ssum * (1.0 / HEAD_DIM) + RMS_EPS)) * gain[:, sl]
        for hh in range(256 // LANES):
            yc = y[:, hh * LANES:(hh + 1) * LANES]
            up = pltpu.roll(yc, LANES - ROT_DIM // 2, axis=1)
            dn = pltpu.roll(yc, ROT_DIM // 2, axis=1)
            rot = yc * cos + jnp.where(first, up, dn) * sin
            lo_c = cc * 256 + hh * LANES
            o_ref[:, lo_c:lo_c + LANES] = rot.astype(o_ref.dtype)


def project(h, w_in, layer, col_blk, ncols, tn, tm, out_dtype, qk=None):
    nj = ncols // tn
    in_specs = [pl.BlockSpec((tm, D_MODEL), lambda j, t: (t, 0)),
                pl.BlockSpec((None, D_MODEL, tn), lambda j, t: (layer, 0, col_blk + j))]
    scratch = [pltpu.VMEM((D_MODEL, tn), BF16)]
    args = [h, w_in]
    if qk is None:
        body = _proj_plain_body
    else:
        gain, cos_t, sin_t = qk
        body = _proj_qk_body
        in_specs += [pl.BlockSpec((1, tn), lambda j, t: (0, j)),
                     pl.BlockSpec((tm, LANES), lambda j, t: (t, 0)),
                     pl.BlockSpec((tm, LANES), lambda j, t: (t, 0))]
        scratch.append(pltpu.VMEM((256, 256), BF16))
        args += [gain, cos_t, sin_t]
    return pl.pallas_call(
        body,
        grid=(nj, T // tm),
        in_specs=in_specs,
        out_specs=pl.BlockSpec((tm, tn), lambda j, t: (t, j)),
        out_shape=jax.ShapeDtypeStruct((T, ncols), out_dtype),
        scratch_shapes=scratch,
        compiler_params=_cp(2, 48),
        name="proj_qk" if qk is not None else "proj",
    )(*args)


def rope_tables():
    pos = jnp.concatenate([jnp.tile(jnp.arange(SEQ, dtype=I32), BATCH),
                           jnp.tile(PAST_LEN + jnp.arange(DEC_SEQ, dtype=I32), DEC_BATCH)])
    inv = ROPE_THETA ** (-jnp.arange(0, ROT_DIM, 2, dtype=F32) / ROT_DIM)
    ang = pos.astype(F32)[:, None] * inv[None, :]
    cos8, sin8 = jnp.cos(ang), jnp.sin(ang)
    rest = HEAD_DIM - ROT_DIM
    c64 = jnp.concatenate([cos8, cos8, jnp.ones((T, rest), F32)], axis=1)
    s64 = jnp.concatenate([-sin8, sin8, jnp.zeros((T, rest), F32)], axis=1)
    return jnp.tile(c64, (1, LANES // HEAD_DIM)), jnp.tile(s64, (1, LANES // HEAD_DIM))


def _lane_shift_variants(a):
    a0, a1 = a[:, :LANES], a[:, LANES:]
    r0 = pltpu.roll(a0, HEAD_DIM, axis=1)
    r1 = pltpu.roll(a1, HEAD_DIM, axis=1)
    low = lax.broadcasted_iota(I32, a0.shape, 1) < HEAD_DIM
    rot = jnp.concatenate([jnp.where(low, r1, r0), jnp.where(low, r0, r1)], axis=1)
    swap = lambda z: jnp.concatenate([z[:, LANES:], z[:, :LANES]], axis=1)
    return [a.astype(BF16), rot.astype(BF16), swap(a).astype(BF16), swap(rot).astype(BF16)]


def _attn_body(sinks_ref, q_ref, kh_ref, km_ref, vh_ref, vm_ref, o_ref, *, qrows, mask_first):
    i = pl.program_id(1)
    k_all = jnp.concatenate([kh_ref[...], km_ref[...]], axis=0)
    v_all = jnp.concatenate([vh_ref[...], vm_ref[...]], axis=0)
    n_groups = q_ref.shape[0] // qrows
    need = (n_groups - 1) * qrows + KEY_WIN
    if k_all.shape[0] < need:
        fill = jnp.zeros((need - k_all.shape[0], KV_W), F32)
        k_all = jnp.concatenate([k_all, fill], axis=0)
        v_all = jnp.concatenate([v_all, fill], axis=0)
    kvar = _lane_shift_variants(k_all)
    vvar = _lane_shift_variants(v_all)
    blk = lax.shift_right_logical(lax.broadcasted_iota(I32, (KEY_WIN, KV_W), 1), 6)
    zero = jnp.zeros((KEY_WIN, KV_W), BF16)
    q_chunk = lax.shift_right_logical(lax.broadcasted_iota(I32, (qrows, KEY_WIN), 0), 6)
    key = lax.broadcasted_iota(I32, (qrows, KEY_WIN), 1)
    band = (key >= q_chunk * CHUNK) & (key < q_chunk * CHUNK + WINDOW + CHUNK)
    for g in range(n_groups):
        w0 = g * qrows
        valid = band
        if mask_first and w0 < WINDOW:
            valid = band & (key >= jnp.where(i == 0, WINDOW - w0, 0))
        for x in range(N_KV_HEADS):
            kcat = jnp.concatenate([jnp.where(blk == j, kvar[(j - x) % GROUP][w0:w0 + KEY_WIN], zero)
                                    for j in range(GROUP)], axis=0)
            vcat = jnp.concatenate([jnp.where(blk == j, vvar[(j - x) % GROUP][w0:w0 + KEY_WIN], zero)
                                    for j in range(GROUP)], axis=0)
            qg = q_ref[w0:w0 + qrows, x * 256:(x + 1) * 256]
            s = lax.dot_general(qg, kcat, (((1,), (1,)), ((), ())),
                                preferred_element_type=F32) * ATTN_SCALE
            probs = []
            for j in range(GROUP):
                sj = jnp.where(valid, s[:, j * KEY_WIN:(j + 1) * KEY_WIN], NEG_INF)
                sink = sinks_ref[x * GROUP + j]
                m = jnp.maximum(jnp.max(sj, axis=1, keepdims=True), sink)
                p = jnp.exp(sj - m)
                den = jnp.sum(p, axis=1, keepdims=True) + jnp.exp(sink - m)
                probs.append((p / den).astype(BF16))
            og = _dot(jnp.concatenate(probs, axis=1), vcat)
            o_ref[w0:w0 + qrows, x * 256:(x + 1) * 256] = og.astype(o_ref.dtype)


def attention_prompt(sinks, q, k, v):
    rows = 4 * CHUNK
    steps = SEQ // rows
    hb = rows // WINDOW
    main = lambda b, i, s: (b * steps + i, 0)
    halo = lambda b, i, s: (jnp.maximum(b * steps * hb + i * hb - 1, b * steps * hb), 0)
    return pl.pallas_call(
        functools.partial(_attn_body, qrows=2 * CHUNK, mask_first=True),
        grid_spec=pltpu.PrefetchScalarGridSpec(
            num_scalar_prefetch=1,
            grid=(BATCH, steps),
            in_specs=[pl.BlockSpec((rows, ATTN_W), main),
                      pl.BlockSpec((WINDOW, KV_W), halo),
                      pl.BlockSpec((rows, KV_W), main),
                      pl.BlockSpec((WINDOW, KV_W), halo),
                      pl.BlockSpec((rows, KV_W), main)],
            out_specs=pl.BlockSpec((rows, ATTN_W), main)),
        out_shape=jax.ShapeDtypeStruct((TP, ATTN_W), BF16),
        compiler_params=_cp(2, 32),
        name="attn_prompt",
    )(sinks, q, k, k, v, v)


def attention_sample(sinks, q, k, v, cache_k, cache_v):
    first = TP // DEC_SEQ
    main = lambda b, i, s: (first + b, 0)
    halo = lambda b, i, s: (b, 0)
    return pl.pallas_call(
        functools.partial(_attn_body, qrows=CHUNK, mask_first=False),
        grid_spec=pltpu.PrefetchScalarGridSpec(
            num_scalar_prefetch=1,
            grid=(DEC_BATCH, 1),
            in_specs=[pl.BlockSpec((DEC_SEQ, ATTN_W), main),
                      pl.BlockSpec((WINDOW, KV_W), halo),
                      pl.BlockSpec((DEC_SEQ, KV_W), main),
                      pl.BlockSpec((WINDOW, KV_W), halo),
                      pl.BlockSpec((DEC_SEQ, KV_W), main)],
            out_specs=pl.BlockSpec((DEC_SEQ, ATTN_W), halo)),
        out_shape=jax.ShapeDtypeStruct((TS, ATTN_W), BF16),
        compiler_params=_cp(2, 32),
        name="attn_sample",
    )(sinks, q, cache_k, k, cache_v, v)


def _pool_body(u_ref, h_ref, pw_ref, sc_ref, y_ref, *, zero_first, full_count):
    i = pl.program_id(1)
    u = u_ref[...]
    tl = u.shape[0]
    halo = h_ref[...]
    if zero_first:
        halo = jnp.where(i == 0, 0.0, halo)
    ext = jnp.concatenate([halo, u], axis=0)
    pos = i * tl + lax.broadcasted_iota(I32, (tl, 1), 0)
    for g, w in enumerate(POOL_WINDOWS):
        sl = slice(g * POOL_GROUP_W, (g + 1) * POOL_GROUP_W)
        s = ext[:, sl]
        sh = 1
        while sh < w:
            s = s + pltpu.roll(s, sh, axis=0)
            sh *= 2
        tot = s[HALO:]
        cnt = float(w) if full_count else jnp.minimum(w, pos + 1).astype(F32)
        d = (tot / cnt - u[:, sl]).astype(BF16)
        yg = _dot(d, pw_ref[g].astype(BF16)) * sc_ref[:, sl]
        y_ref[:, sl] = yg.astype(y_ref.dtype)


def pool_prompt(u, pool_w, pool_scale, layer):
    tl = 512
    steps = SEQ // tl
    hpb = tl // HALO
    main = lambda b, i: (b * steps + i, 0)
    halo = lambda b, i: (jnp.maximum((b * steps + i) * hpb - 1, b * steps * hpb), 0)
    return pl.pallas_call(
        functools.partial(_pool_body, zero_first=True, full_count=False),
        grid=(BATCH, steps),
        in_specs=[pl.BlockSpec((tl, POOL_W), main),
                  pl.BlockSpec((HALO, POOL_W), halo),
                  pl.BlockSpec((None, 4, POOL_GROUP_W, POOL_GROUP_W), lambda b, i: (layer, 0, 0, 0)),
                  pl.BlockSpec((1, POOL_W), lambda b, i: (0, 0))],
        out_specs=pl.BlockSpec((tl, POOL_W), main),
        out_shape=jax.ShapeDtypeStruct((TP, POOL_W), BF16),
        compiler_params=_cp(2, 32),
        name="pool_prompt",
    )(u, u, pool_w, pool_scale)


def pool_sample(u, hist, pool_w, pool_scale, layer):
    first = TP // DEC_SEQ
    return pl.pallas_call(
        functools.partial(_pool_body, zero_first=False, full_count=True),
        grid=(DEC_BATCH, 1),
        in_specs=[pl.BlockSpec((DEC_SEQ, POOL_W), lambda b, i: (first + b, 0)),
                  pl.BlockSpec((HALO, POOL_W), lambda b, i: (b, 0)),
                  pl.BlockSpec((None, 4, POOL_GROUP_W, POOL_GROUP_W), lambda b, i: (layer, 0, 0, 0)),
                  pl.BlockSpec((1, POOL_W), lambda b, i: (0, 0))],
        out_specs=pl.BlockSpec((DEC_SEQ, POOL_W), lambda b, i: (b, 0)),
        out_shape=jax.ShapeDtypeStruct((TS, POOL_W), BF16),
        compiler_params=_cp(2, 32),
        name="pool_sample",
    )(u, hist, pool_w, pool_scale)


def _branch_body(h_ref, op_ref, os_ref, yp_ref, ys_ref, wga_ref, wgb_ref, wa_ref, wp_ref, out_ref,
                 wga, wgb, wa, wp):
    t = pl.program_id(1)

    @pl.when(t == 0)
    def _():
        wga[...] = wga_ref[...].astype(BF16)
        wgb[...] = wgb_ref[...].astype(BF16)
        wa[...] = wa_ref[...].astype(BF16)
        wp[...] = wp_ref[...].astype(BF16)

    h = h_ref[...]
    ga = jax.nn.sigmoid(_dot(h, wga[...]))
    gb = jax.nn.sigmoid(_dot(h, wgb[...]))
    a = _dot(_load_rows((op_ref, os_ref), t), wa[...])
    c = _dot(_load_rows((yp_ref, ys_ref), t), wp[...])
    out_ref[...] = (ga * a + gb * c).astype(out_ref.dtype)


def branch_mix(h, o_p, o_s, y_p, y_s, w_in, w_attn_br, w_pool_br, layer):
    tn, tm = 512, 512
    ga_blk = (ATTN_W + 2 * KV_W + POOL_W) // tn
    gb_blk = ga_blk + D_MODEL // tn
    return pl.pallas_call(
        _branch_body,
        grid=(D_MODEL // tn, T // tm),
        in_specs=([pl.BlockSpec((tm, D_MODEL), lambda j, t: (t, 0))]
                  + _row_specs(2, tm, ATTN_W, 2) + _row_specs(2, tm, POOL_W, 2)
                  + [pl.BlockSpec((None, D_MODEL, tn), lambda j, t: (layer, 0, ga_blk + j)),
                     pl.BlockSpec((None, D_MODEL, tn), lambda j, t: (layer, 0, gb_blk + j)),
                     pl.BlockSpec((None, ATTN_W, tn), lambda j, t: (layer, 0, j)),
                     pl.BlockSpec((None, POOL_W, tn), lambda j, t: (layer, 0, j))]),
        out_specs=pl.BlockSpec((tm, tn), lambda j, t: (t, j)),
        out_shape=jax.ShapeDtypeStruct((T, D_MODEL), BF16),
        scratch_shapes=[pltpu.VMEM((D_MODEL, tn), BF16), pltpu.VMEM((D_MODEL, tn), BF16),
                        pltpu.VMEM((ATTN_W, tn), BF16), pltpu.VMEM((POOL_W, tn), BF16)],
        compiler_params=_cp(2, 56),
        name="branch_mix",
    )(h, o_p, o_s, y_p, y_s, w_in, w_in, w_attn_br, w_pool_br)


def _wout_body(*refs, n_x, router):
    m_ref, *x_refs = refs[:1 + n_x]
    rest = refs[1 + n_x:]
    if router:
        w_ref, g_ref, r_ref, x1_ref, h2_ref, lg_ref, wbf = rest
    else:
        w_ref, g_ref, x1_ref, h2_ref, wbf = rest
    t = pl.program_id(0)

    @pl.when(t == 0)
    def _():
        wbf[...] = w_ref[...].astype(BF16)

    x1 = _load_rows(x_refs, t) + _dot(m_ref[...], wbf[...])
    x1_ref[...] = x1
    h2 = _rms(x1, g_ref[...])
    h2_ref[...] = h2.astype(h2_ref.dtype)
    if router:
        r = r_ref[...]
        r_hi = r.astype(BF16)
        r_lo = (r - r_hi.astype(F32)).astype(BF16)
        h_hi = h2.astype(BF16)
        h_lo = (h2 - h_hi.astype(F32)).astype(BF16)
        lg_ref[...] = _dot(h_hi, r_hi) + (_dot(h_lo, r_hi) + _dot(h_hi, r_lo))


def out_proj(mixed, x_parts, w_out, g, layer, router_w=None):
    tm = 256
    router = router_w is not None
    row = lambda t: (t, 0)
    const2 = lambda t: (0, 0)
    in_specs = ([pl.BlockSpec((tm, D_MODEL), row)] + _row_specs(len(x_parts), tm, D_MODEL, 1)
                + [pl.BlockSpec((None, D_MODEL, D_MODEL), lambda t: (layer, 0, 0),
                                pipeline_mode=pl.Buffered(1)),
                   pl.BlockSpec((1, D_MODEL), const2)])
    out_specs = [pl.BlockSpec((tm, D_MODEL), row), pl.BlockSpec((tm, D_MODEL), row)]
    out_shape = [jax.ShapeDtypeStruct((T, D_MODEL), F32),
                 jax.ShapeDtypeStruct((T, D_MODEL), F32 if router else BF16)]
    args = [mixed, *x_parts, w_out, g.reshape(1, D_MODEL)]
    if router:
        in_specs.append(pl.BlockSpec((D_MODEL, LANES), const2))
        out_specs.append(pl.BlockSpec((tm, LANES), row))
        out_shape.append(jax.ShapeDtypeStruct((T, LANES), F32))
        args.append(router_w)
    return pl.pallas_call(
        functools.partial(_wout_body, n_x=len(x_parts), router=router),
        grid=(T // tm,),
        in_specs=in_specs,
        out_specs=out_specs,
        out_shape=out_shape,
        scratch_shapes=[pltpu.VMEM((D_MODEL, D_MODEL), BF16)],
        compiler_params=_cp(1, 56),
        name="out_proj",
    )(*args)


def _ffn_up_body(h_ref, wg_ref, wu_ref, o_ref, wg, wu):
    @pl.when(pl.program_id(1) == 0)
    def _():
        wg[...] = wg_ref[...].astype(BF16)
        wu[...] = wu_ref[...].astype(BF16)

    h = h_ref[...]
    g = _dot(h, wg[...])
    o_ref[...] = (jax.nn.silu(g) * _dot(h, wu[...])).astype(o_ref.dtype)


def ffn_up(h2, w_gate_up, idx):
    tm = 1024
    return pl.pallas_call(
        _ffn_up_body,
        grid=(N_FF_BLK, T // tm),
        in_specs=[pl.BlockSpec((tm, D_MODEL), lambda j, t: (t, 0)),
                  pl.BlockSpec((None, D_MODEL, FF_BLK), lambda j, t: (idx, 0, j)),
                  pl.BlockSpec((None, D_MODEL, FF_BLK), lambda j, t: (idx, 0, N_FF_BLK + j))],
        out_specs=pl.BlockSpec((tm, FF_BLK), lambda j, t: (t, j)),
        out_shape=jax.ShapeDtypeStruct((T, D_FF), BF16),
        scratch_shapes=[pltpu.VMEM((D_MODEL, FF_BLK), BF16), pltpu.VMEM((D_MODEL, FF_BLK), BF16)],
        compiler_params=_cp(2, 48),
        name="ffn_up",
    )(h2, w_gate_up, w_gate_up)


def _ffn_down_body(a_ref, w_ref, x_ref, o_ref, wbf):
    @pl.when(pl.program_id(1) == 0)
    def _():
        wbf[...] = w_ref[...].astype(BF16)

    o_ref[...] = x_ref[...] + _dot(a_ref[...], wbf[...])


def ffn_down(act, w_down, idx, x1):
    tm, tn = 512, 512
    return pl.pallas_call(
        _ffn_down_body,
        grid=(D_MODEL // tn, T // tm),
        in_specs=[pl.BlockSpec((tm, D_FF), lambda j, t: (t, 0)),
                  pl.BlockSpec((None, D_FF, tn), lambda j, t: (idx, 0, j)),
                  pl.BlockSpec((tm, tn), lambda j, t: (t, j))],
        out_specs=pl.BlockSpec((tm, tn), lambda j, t: (t, j)),
        out_shape=jax.ShapeDtypeStruct((T, D_MODEL), F32),
        scratch_shapes=[pltpu.VMEM((D_FF, tn), BF16)],
        compiler_params=_cp(2, 56),
        name="ffn_down",
    )(act, w_down, x1)


def _ple_body(h_ref, pp_ref, ps_ref, x_ref, wg_ref, wp_ref, *rest, last):
    t = pl.program_id(0)
    if last:
        yp_ref, ys_ref, wg, wp = rest
    else:
        gn_ref, x3_ref, hn_ref, wg, wp = rest

    @pl.when(t == 0)
    def _():
        wg[...] = wg_ref[...].astype(BF16)
        wp[...] = wp_ref[...].astype(BF16)

    gate = jax.nn.sigmoid(_dot(h_ref[...], wg[...]))
    p = _load_rows((pp_ref, ps_ref), t).astype(BF16)
    x3 = x_ref[...] + _dot(p, wp[...]) * gate
    if last:
        pb = TP // x3.shape[0]

        @pl.when(t < pb)
        def _():
            yp_ref[...] = x3

        @pl.when(t >= pb)
        def _():
            ys_ref[...] = x3
    else:
        x3_ref[...] = x3
        hn_ref[...] = _rms(x3, gn_ref[...]).astype(hn_ref.dtype)


def ple(h3, p_prompt, p_sample, x2, w_ple_gate, w_ple, layer, g_next):
    tm = 256
    last = g_next is None
    pb = TP // tm
    row = lambda t: (t, 0)
    prow = lambda t: (jnp.minimum(t, pb - 1), 0)
    srow = lambda t: (jnp.maximum(t - pb, 0), 0)
    in_specs = [pl.BlockSpec((tm, D_MODEL), row),
                pl.BlockSpec((None, tm, PLE_DIM), lambda t: (layer, jnp.minimum(t, pb - 1), 0)),
                pl.BlockSpec((None, tm, PLE_DIM), lambda t: (layer, jnp.maximum(t - pb, 0), 0)),
                pl.BlockSpec((tm, D_MODEL), row),
                pl.BlockSpec((None, D_MODEL, D_MODEL), lambda t: (layer, 0, 0), pipeline_mode=pl.Buffered(1)),
                pl.BlockSpec((None, PLE_DIM, D_MODEL), lambda t: (layer, 0, 0), pipeline_mode=pl.Buffered(1))]
    args = [h3, p_prompt.reshape(DEPTH, TP, PLE_DIM), p_sample.reshape(DEPTH, TS, PLE_DIM), x2, w_ple_gate, w_ple]
    if last:
        out_specs = [pl.BlockSpec((tm, D_MODEL), prow), pl.BlockSpec((tm, D_MODEL), srow)]
        out_shape = [jax.ShapeDtypeStruct((TP, D_MODEL), F32), jax.ShapeDtypeStruct((TS, D_MODEL), F32)]
    else:
        in_specs.append(pl.BlockSpec((1, D_MODEL), lambda t: (0, 0)))
        args.append(g_next.reshape(1, D_MODEL))
        out_specs = [pl.BlockSpec((tm, D_MODEL), row), pl.BlockSpec((tm, D_MODEL), row)]
        out_shape = [jax.ShapeDtypeStruct((T, D_MODEL), F32), jax.ShapeDtypeStruct((T, D_MODEL), BF16)]
    return pl.pallas_call(
        functools.partial(_ple_body, last=last),
        grid=(T // tm,),
        in_specs=in_specs,
        out_specs=out_specs,
        out_shape=out_shape,
        scratch_shapes=[pltpu.VMEM((D_MODEL, D_MODEL), BF16), pltpu.VMEM((PLE_DIM, D_MODEL), BF16)],
        compiler_params=_cp(1, 56),
        name="ple",
    )(*args)


def _router_body(lg_ref, meta_ref, cnt_ref, carry):
    @pl.when(pl.program_id(0) == 0)
    def _():
        carry[...] = jnp.zeros_like(carry)

    lg = lg_ref[...]
    tm = lg.shape[0]
    lane = lax.broadcasted_iota(I32, (tm, LANES), 1)
    lane_f = lane.astype(F32)
    big = float(LANES)
    l1 = jnp.where(lane < N_EXPERTS, lg, -jnp.inf)
    m1 = jnp.max(l1, axis=1, keepdims=True)
    i1 = jnp.min(jnp.where(l1 == m1, lane_f, big), axis=1, keepdims=True)
    oh1 = lane_f == i1
    l2 = jnp.where(oh1, -jnp.inf, l1)
    m2 = jnp.max(l2, axis=1, keepdims=True)
    i2 = jnp.min(jnp.where(l2 == m2, lane_f, big), axis=1, keepdims=True)
    oh2 = lane_f == i2
    e = jnp.exp(m2 - m1)
    w1 = 1.0 / (1.0 + e)
    w2 = e / (1.0 + e)
    oh = jnp.where(oh1 | oh2, 1.0, 0.0)
    r = lax.broadcasted_iota(I32, (tm, tm), 0)
    c = lax.broadcasted_iota(I32, (tm, tm), 1)
    tri = jnp.where(c < r, 1.0, 0.0).astype(BF16)
    before = _dot(tri, oh.astype(BF16)) + carry[0:1, :]
    r1 = jnp.sum(jnp.where(oh1, before, 0.0), axis=1, keepdims=True)
    r2 = jnp.sum(jnp.where(oh2, before, 0.0), axis=1, keepdims=True)
    meta = jnp.where(lane == 0, i1, jnp.where(lane == 1, i2, jnp.where(lane == 2, w1,
           jnp.where(lane == 3, w2, jnp.where(lane == 4, r1, jnp.where(lane == 5, r2, 0.0))))))
    meta_ref[...] = meta
    total = carry[0:1, :] + jnp.sum(oh, axis=0, keepdims=True)
    carry[...] = jnp.broadcast_to(total, carry.shape)
    cnt_ref[...] = jnp.broadcast_to(total, cnt_ref.shape)


def route(logits):
    tm = 512
    return pl.pallas_call(
        _router_body,
        grid=(T // tm,),
        in_specs=[pl.BlockSpec((tm, LANES), lambda t: (t, 0))],
        out_specs=[pl.BlockSpec((tm, LANES), lambda t: (t, 0)),
                   pl.BlockSpec((8, LANES), lambda t: (0, 0))],
        out_shape=[jax.ShapeDtypeStruct((T, LANES), F32), jax.ShapeDtypeStruct((8, LANES), F32)],
        scratch_shapes=[pltpu.VMEM((8, LANES), F32)],
        compiler_params=_cp(1, 32),
        name="moe_route",
    )(logits)


def moe_plan(meta, cnt_rows):
    i1 = meta[:, 0].astype(I32)
    i2 = meta[:, 1].astype(I32)
    r1 = meta[:, 4].astype(I32)
    r2 = meta[:, 5].astype(I32)
    cnt = cnt_rows[0, :N_EXPERTS].astype(I32)
    tiles_e = (cnt + MOE_TILE - 1) // MOE_TILE
    cum_tiles = jnp.cumsum(tiles_e)
    pad_off = (cum_tiles - tiles_e) * MOE_TILE
    n_tiles = cum_tiles[-1:]
    e_ids = jnp.arange(N_EXPERTS, dtype=I32)
    pick = lambda idx: jnp.sum(jnp.where(idx[:, None] == e_ids[None, :], pad_off[None, :], 0), axis=1)
    pos1 = pick(i1) + r1
    pos2 = pick(i2) + r2
    last_e = jnp.max(jnp.where(cnt > 0, e_ids, 0))
    tile_ids = jnp.arange(MOE_TILES, dtype=I32)
    tile_e = jnp.minimum(jnp.sum(tile_ids[:, None] >= cum_tiles[None, :], axis=1).astype(I32), last_e)
    pad_rows = (pad_off + cnt)[:, None] + jnp.arange(MOE_TILE - 1, dtype=I32)[None, :]
    pad_keys = jnp.where(pad_rows < (pad_off + tiles_e * MOE_TILE)[:, None], pad_rows, 2 * MOE_ROWS).reshape(-1)
    tok = jnp.arange(T, dtype=I32)
    keys = jnp.concatenate([pos1, pos2, pad_keys])
    vals = jnp.concatenate([tok, tok, jnp.zeros((MOE_PAD,), I32)])
    row_tok = lax.sort((keys, vals), num_keys=1)[1][:MOE_ROWS]
    return pos1, pos2, tile_e, n_tiles, row_tok


def _row_copy(src_hbm, dst_vmem, sem, src_row, dst_row):
    return pltpu.make_async_copy(src_hbm.at[pl.ds(src_row, 1)], dst_vmem.at[pl.ds(dst_row, 1)], sem)


def _start_rows(idx_ref, src_hbm, dst_vmem, sem):
    def body(r, carry):
        _row_copy(src_hbm, dst_vmem, sem, idx_ref[0, 0, r], r).start()
        return carry

    lax.fori_loop(0, dst_vmem.shape[0], body, 0, unroll=8)


def _wait_rows(src_hbm, dst_vmem, sem):
    def body(r, carry):
        _row_copy(src_hbm, dst_vmem, sem, 0, r).wait()
        return carry

    lax.fori_loop(0, dst_vmem.shape[0], body, 0, unroll=8)


def _zero_dead_tile(nt_ref, r, o_ref):
    @pl.when(r >= nt_ref[0])
    def _():
        o_ref[...] = jnp.zeros_like(o_ref)


def _dispatch_body(nt_ref, tok_ref, nxt_ref, h_hbm, o_ref, buf, sem):
    r = pl.program_id(0)
    slot = r % 2

    @pl.when(r == 0)
    def _():
        _start_rows(tok_ref, h_hbm, buf.at[0], sem.at[0])

    @pl.when(r + 1 < nt_ref[0])
    def _():
        _start_rows(nxt_ref, h_hbm, buf.at[1 - slot], sem.at[1 - slot])

    @pl.when(r < nt_ref[0])
    def _():
        _wait_rows(h_hbm, buf.at[slot], sem.at[slot])
        o_ref[...] = buf[slot].astype(o_ref.dtype)

    _zero_dead_tile(nt_ref, r, o_ref)


def moe_dispatch(n_tiles, row_tok, h2f):
    smem = lambda off: pl.BlockSpec((1, 1, MOE_TILE),
                                    lambda r, nt: (jnp.minimum(r + off, MOE_TILES - 1), 0, 0),
                                    memory_space=pltpu.SMEM)
    tok3 = row_tok.reshape(MOE_TILES, 1, MOE_TILE)
    return pl.pallas_call(
        _dispatch_body,
        grid_spec=pltpu.PrefetchScalarGridSpec(
            num_scalar_prefetch=1,
            grid=(MOE_TILES,),
            in_specs=[smem(0), smem(1), pl.BlockSpec(memory_space=pl.ANY)],
            out_specs=pl.BlockSpec((MOE_TILE, D_MODEL), lambda r, nt: (r, 0)),
            scratch_shapes=[pltpu.VMEM((2, MOE_TILE, D_MODEL), F32), pltpu.SemaphoreType.DMA((2,))]),
        out_shape=jax.ShapeDtypeStruct((MOE_ROWS, D_MODEL), BF16),
        compiler_params=_cp(1, 32),
        name="moe_dispatch",
    )(n_tiles, tok3, tok3, h2f)


def _expert_changed(te_ref, r):
    return jnp.logical_or(r == 0, te_ref[r] != te_ref[jnp.maximum(r - 1, 0)])


def _moe_up_body(te_ref, nt_ref, a_ref, wg_ref, wu_ref, o_ref, wg, wu):
    r = pl.program_id(1)

    @pl.when(r < nt_ref[0])
    def _():
        @pl.when(_expert_changed(te_ref, r))
        def _():
            wg[...] = wg_ref[...].astype(BF16)
            wu[...] = wu_ref[...].astype(BF16)

        a = a_ref[...]
        g = _dot(a, wg[...])
        o_ref[...] = (jax.nn.silu(g) * _dot(a, wu[...])).astype(o_ref.dtype)

    _zero_dead_tile(nt_ref, r, o_ref)


def moe_up(tile_e, n_tiles, xs, w_gate_up, idx):
    live = lambda j, r, te, nt: (jnp.minimum(r, nt[0] - 1), 0)
    return pl.pallas_call(
        _moe_up_body,
        grid_spec=pltpu.PrefetchScalarGridSpec(
            num_scalar_prefetch=2,
            grid=(N_FF_BLK, MOE_TILES),
            in_specs=[pl.BlockSpec((MOE_TILE, D_MODEL), live),
                      pl.BlockSpec((None, None, D_MODEL, FF_BLK),
                                   lambda j, r, te, nt: (idx, te[r], 0, j)),
                      pl.BlockSpec((None, None, D_MODEL, FF_BLK),
                                   lambda j, r, te, nt: (idx, te[r], 0, N_FF_BLK + j))],
            out_specs=pl.BlockSpec((MOE_TILE, FF_BLK), lambda j, r, te, nt: (r, j)),
            scratch_shapes=[pltpu.VMEM((D_MODEL, FF_BLK), BF16), pltpu.VMEM((D_MODEL, FF_BLK), BF16)]),
        out_shape=jax.ShapeDtypeStruct((MOE_ROWS, D_FF), BF16),
        compiler_params=_cp(2, 48),
        name="moe_up",
    )(tile_e, n_tiles, xs, w_gate_up, w_gate_up)


def _moe_down_body(te_ref, nt_ref, a_ref, w_ref, o_ref, wbf):
    r = pl.program_id(1)

    @pl.when(r < nt_ref[0])
    def _():
        @pl.when(_expert_changed(te_ref, r))
        def _():
            wbf[...] = w_ref[...].astype(BF16)

        o_ref[...] = _dot(a_ref[...], wbf[...])

    _zero_dead_tile(nt_ref, r, o_ref)


def moe_down(tile_e, n_tiles, act, w_down, idx):
    tn = 512
    return pl.pallas_call(
        _moe_down_body,
        grid_spec=pltpu.PrefetchScalarGridSpec(
            num_scalar_prefetch=2,
            grid=(D_MODEL // tn, MOE_TILES),
            in_specs=[pl.BlockSpec((MOE_TILE, D_FF), lambda j, r, te, nt: (jnp.minimum(r, nt[0] - 1), 0)),
                      pl.BlockSpec((None, None, D_FF, tn), lambda j, r, te, nt: (idx, te[r], 0, j))],
            out_specs=pl.BlockSpec((MOE_TILE, tn), lambda j, r, te, nt: (r, j)),
            scratch_shapes=[pltpu.VMEM((D_FF, tn), BF16)]),
        out_shape=jax.ShapeDtypeStruct((MOE_ROWS, D_MODEL), F32),
        compiler_params=_cp(2, 56),
        name="moe_down",
    )(tile_e, n_tiles, act, w_down)


def _combine_body(p1_ref, p2_ref, n1_ref, n2_ref, ys_hbm, meta_ref, x_ref, g_ref, x2_ref, h3_ref,
                  b1, b2, sem1, sem2):
    t = pl.program_id(0)
    slot = t % 2

    @pl.when(t == 0)
    def _():
        _start_rows(p1_ref, ys_hbm, b1.at[0], sem1.at[0])
        _start_rows(p2_ref, ys_hbm, b2.at[0], sem2.at[0])

    @pl.when(t + 1 < pl.num_programs(0))
    def _():
        _start_rows(n1_ref, ys_hbm, b1.at[1 - slot], sem1.at[1 - slot])
        _start_rows(n2_ref, ys_hbm, b2.at[1 - slot], sem2.at[1 - slot])

    _wait_rows(ys_hbm, b1.at[slot], sem1.at[slot])
    _wait_rows(ys_hbm, b2.at[slot], sem2.at[slot])
    meta = meta_ref[...]
    w1 = meta[:, 2:3]
    w2 = meta[:, 3:4]
    x2 = x_ref[...] + (w1 * b1[slot] + w2 * b2[slot])
    x2_ref[...] = x2
    h3_ref[...] = _rms(x2, g_ref[...]).astype(h3_ref.dtype)


def moe_combine(pos1, pos2, ys, meta, x1, g):
    tm = 256
    nblk = T // tm
    row = lambda t: (t, 0)
    smem = lambda off: pl.BlockSpec((1, 1, tm), lambda t: (jnp.minimum(t + off, nblk - 1), 0, 0),
                                    memory_space=pltpu.SMEM)
    p1 = pos1.reshape(nblk, 1, tm)
    p2 = pos2.reshape(nblk, 1, tm)
    return pl.pallas_call(
        _combine_body,
        grid=(nblk,),
        in_specs=[smem(0), smem(0), smem(1), smem(1),
                  pl.BlockSpec(memory_space=pl.ANY),
                  pl.BlockSpec((tm, LANES), row),
                  pl.BlockSpec((tm, D_MODEL), row),
                  pl.BlockSpec((1, D_MODEL), lambda t: (0, 0))],
        out_specs=[pl.BlockSpec((tm, D_MODEL), row), pl.BlockSpec((tm, D_MODEL), row)],
        out_shape=[jax.ShapeDtypeStruct((T, D_MODEL), F32), jax.ShapeDtypeStruct((T, D_MODEL), BF16)],
        scratch_shapes=[pltpu.VMEM((2, tm, D_MODEL), F32), pltpu.VMEM((2, tm, D_MODEL), F32),
                        pltpu.SemaphoreType.DMA((2,)), pltpu.SemaphoreType.DMA((2,))],
        compiler_params=_cp(1, 48),
        name="moe_combine",
    )(p1, p2, p1, p2, ys, meta, x1, g.reshape(1, D_MODEL))


def _last_rows(a, batch, seq, n):
    return a.reshape(batch, seq, a.shape[-1])[:, seq - n:]


def kernel(x_prompt, x_sample, cache_k, cache_v, state_pool, p_prompt, p_sample, norm_mix, w_in, q_norm, k_norm, sinks, pool_w, pool_scale, w_attn_br, w_pool_br, w_out, norm_ffn, ffn_w_gate_up, ffn_w_down, moe_router, moe_w_gate_up, moe_w_down, norm_ple, w_ple, w_ple_gate):
    x_parts = (x_prompt.reshape(TP, D_MODEL), x_sample.reshape(TS, D_MODEL))
    cos_t, sin_t = rope_tables()
    outs = {k: [] for k in ("kp", "vp", "pp", "ks", "vs", "ps")}
    h = rmsnorm_bf16(x_parts, norm_mix[0])

    for l in range(DEPTH):
        q_gain = jnp.tile(q_norm[l], ATTN_W // HEAD_DIM).reshape(1, ATTN_W)
        k_gain = jnp.tile(k_norm[l], KV_W // HEAD_DIM).reshape(1, KV_W)
        q = project(h, w_in, l, 0, ATTN_W, 512, 512, BF16, qk=(q_gain, cos_t, sin_t))
        k = project(h, w_in, l, ATTN_W // KV_W, KV_W, KV_W, 512, F32, qk=(k_gain, cos_t, sin_t))
        v = project(h, w_in, l, (ATTN_W + KV_W) // KV_W, KV_W, KV_W, 1024, F32)
        u = project(h, w_in, l, (ATTN_W + 2 * KV_W) // 512, POOL_W, 512, 1024, F32)

        ck = cache_k[l].reshape(DEC_BATCH * WINDOW, KV_W)
        cv = cache_v[l].reshape(DEC_BATCH * WINDOW, KV_W)
        o_p = attention_prompt(sinks[l], q, k, v)
        o_s = attention_sample(sinks[l], q, k, v, ck, cv)

        hist = jnp.pad(state_pool[l], ((0, 0), (HALO - POOL_HIST, 0), (0, 0))).reshape(DEC_BATCH * HALO, POOL_W)
        scale = pool_scale[l].reshape(1, POOL_W)
        y_p = pool_prompt(u, pool_w, scale, l)
        y_s = pool_sample(u, hist, pool_w, scale, l)

        mixed = branch_mix(h, o_p, o_s, y_p, y_s, w_in, w_attn_br, w_pool_br, l)

        if l % 2 == 0:
            x1, h2 = out_proj(mixed, x_parts, w_out, norm_ffn[l], l)
            act = ffn_up(h2, ffn_w_gate_up, l // 2)
            x2 = ffn_down(act, ffn_w_down, l // 2, x1)
            h3 = rmsnorm_bf16((x2,), norm_ple[l])
        else:
            router_w = jnp.pad(moe_router[l // 2], ((0, 0), (0, LANES - N_EXPERTS)))
            x1, h2f, logits = out_proj(mixed, x_parts, w_out, norm_ffn[l], l, router_w=router_w)
            meta, cnt_rows = route(logits)
            pos1, pos2, tile_e, n_tiles, row_tok = moe_plan(meta, cnt_rows)
            xs = moe_dispatch(n_tiles, row_tok, h2f)
            act = moe_up(tile_e, n_tiles, xs, moe_w_gate_up, l // 2)
            ys = moe_down(tile_e, n_tiles, act, moe_w_down, l // 2)
            x2, h3 = moe_combine(pos1, pos2, ys, meta, x1, norm_ple[l])

        if l + 1 < DEPTH:
            x, h = ple(h3, p_prompt, p_sample, x2, w_ple_gate, w_ple, l, norm_mix[l + 1])
            x_parts = (x,)
        else:
            y_prompt, y_sample = ple(h3, p_prompt, p_sample, x2, w_ple_gate, w_ple, l, None)

        kv_shape = (N_KV_HEADS, HEAD_DIM)
        outs["kp"].append(_last_rows(k[:TP], BATCH, SEQ, WINDOW).reshape(BATCH, WINDOW, *kv_shape))
        outs["vp"].append(_last_rows(v[:TP], BATCH, SEQ, WINDOW).reshape(BATCH, WINDOW, *kv_shape))
        ksn = k[TP:].reshape(DEC_BATCH, DEC_SEQ, *kv_shape)
        vsn = v[TP:].reshape(DEC_BATCH, DEC_SEQ, *kv_shape)
        outs["ks"].append(jnp.concatenate([cache_k[l][:, DEC_SEQ:], ksn], axis=1))
        outs["vs"].append(jnp.concatenate([cache_v[l][:, DEC_SEQ:], vsn], axis=1))
        outs["pp"].append(_last_rows(u[:TP], BATCH, SEQ, POOL_HIST))
        us = u[TP:].reshape(DEC_BATCH, DEC_SEQ, POOL_W)
        outs["ps"].append(jnp.concatenate([state_pool[l], us], axis=1)[:, -POOL_HIST:])

    return (y_prompt.reshape(BATCH, SEQ, D_MODEL), y_sample.reshape(DEC_BATCH, DEC_SEQ, D_MODEL),
            jnp.stack(outs["kp"]), jnp.stack(outs["vp"]), jnp.stack(outs["pp"]),
            jnp.stack(outs["ks"]), jnp.stack(outs["vs"]), jnp.stack(outs["ps"]))
```

```python
import functools

import jax
import jax.numpy as jnp
from jax import lax
from jax.experimental import pallas as pl
from jax.experimental.pallas import tpu as pltpu

F32 = jnp.float32
BF16 = jnp.bfloat16
I32 = jnp.int32

D_MODEL = 2048
BATCH = 2
SEQ = 4096
DEPTH = 2
DEC_BATCH = 16
DEC_SEQ = 64
PAST_LEN = 2048
CHUNK = 64
N_HEADS = 16
N_KV_HEADS = 4
HEAD_DIM = 64
GROUP = N_HEADS // N_KV_HEADS
WINDOW = 128
ROT_DIM = HEAD_DIM // 4
ROPE_THETA = 500000.0
ATTN_SCALE = HEAD_DIM ** -0.5
POOL_WINDOWS = (2, 4, 8, 16)
POOL_W = 1024
POOL_GROUP_W = 256
POOL_HIST = 15
ATTN_W = N_HEADS * HEAD_DIM
KV_W = N_KV_HEADS * HEAD_DIM
IN_W = ATTN_W + 2 * KV_W + POOL_W + 2 * D_MODEL
D_FF = 5632
N_EXPERTS = 8
PLE_DIM = 256
RMS_EPS = 1e-6
NEG_INF = -1e30

TP = BATCH * SEQ
TS = DEC_BATCH * DEC_SEQ
T = TP + TS

LANES = 128
HALO = 16
KEY_WIN = 256
MOE_TILE = 512
MOE_PAD = N_EXPERTS * (MOE_TILE - 1)
MOE_TILES = (2 * T + MOE_PAD) // MOE_TILE
MOE_ROWS = MOE_TILES * MOE_TILE
FF_BLK = 512
N_FF_BLK = D_FF // FF_BLK


def _cp(n_axes, vmem_mb):
    return pltpu.CompilerParams(dimension_semantics=("arbitrary",) * n_axes,
                                vmem_limit_bytes=vmem_mb * 2 ** 20)


def _dot(a, b):
    return jnp.dot(a, b, preferred_element_type=F32)


def _rms(x, g):
    ms = jnp.mean(x * x, axis=-1, keepdims=True)
    return (x * lax.rsqrt(ms + RMS_EPS)) * g


def _row_specs(n_parts, tm, width, nd):
    if n_parts == 1:
        return [pl.BlockSpec((tm, width), lambda *g: (g[nd - 1], 0))]
    pb = TP // tm
    return [pl.BlockSpec((tm, width), lambda *g: (jnp.minimum(g[nd - 1], pb - 1), 0)),
            pl.BlockSpec((tm, width), lambda *g: (jnp.maximum(g[nd - 1] - pb, 0), 0))]


def _load_rows(refs, t):
    if len(refs) == 1:
        return refs[0][...]
    tm = refs[0].shape[0]
    return jnp.where(t < TP // tm, refs[0][...], refs[1][...])


def _rmsnorm_body(*refs):
    *x_refs, g_ref, o_ref = refs
    o_ref[...] = _rms(_load_rows(x_refs, pl.program_id(0)), g_ref[...]).astype(o_ref.dtype)


def rmsnorm_bf16(x_parts, g):
    tm = 512
    return pl.pallas_call(
        _rmsnorm_body,
        grid=(T // tm,),
        in_specs=_row_specs(len(x_parts), tm, D_MODEL, 1) + [pl.BlockSpec((1, D_MODEL), lambda i: (0, 0))],
        out_specs=pl.BlockSpec((tm, D_MODEL), lambda i: (i, 0)),
        out_shape=jax.ShapeDtypeStruct((T, D_MODEL), BF16),
        compiler_params=_cp(1, 32),
        name="rmsnorm",
    )(*x_parts, g.reshape(1, D_MODEL))


def _proj_plain_body(a_ref, w_ref, o_ref, wbf):
    @pl.when(pl.program_id(1) == 0)
    def _():
        wbf[...] = w_ref[...].astype(BF16)

    o_ref[...] = _dot(a_ref[...], wbf[...]).astype(o_ref.dtype)


def _proj_qk_body(a_ref, w_ref, gain_ref, cos_ref, sin_ref, o_ref, wbf, seg):
    @pl.when(pl.program_id(1) == 0)
    def _():
        wbf[...] = w_ref[...].astype(BF16)
        r = lax.shift_right_logical(lax.broadcasted_iota(I32, seg.shape, 0), 6)
        c = lax.shift_right_logical(lax.broadcasted_iota(I32, seg.shape, 1), 6)
        seg[...] = jnp.where(r == c, 1.0, 0.0).astype(BF16)

    acc = _dot(a_ref[...], wbf[...])
    tm, tn = acc.shape
    sq = acc * acc
    hi = sq.astype(BF16)
    lo = (sq - hi.astype(F32)).astype(BF16)
    gain = gain_ref[...]
    cos = cos_ref[...]
    sin = sin_ref[...]
    first = (lax.broadcasted_iota(I32, (tm, LANES), 1) & (HEAD_DIM - 1)) < ROT_DIM // 2
    for cc in range(tn // 256):
        sl = slice(cc * 256, (cc + 1) * 256)
        ssum = _dot(hi[:, sl], seg[...]) + _dot(lo[:, sl], seg[...])
        y = (acc[:, sl] * lax.rsqrt(ssum * (1.0 / HEAD_DIM) + RMS_EPS)) * gain[:, sl]
        for hh in range(256 // LANES):
            yc = y[:, hh * LANES:(hh + 1) * LANES]
            up = pltpu.roll(yc, LANES - ROT_DIM // 2, axis=1)
            dn = pltpu.roll(yc, ROT_DIM // 2, axis=1)
            rot = yc * cos + jnp.where(first, up, dn) * sin
            lo_c = cc * 256 + hh * LANES
            o_ref[:, lo_c:lo_c + LANES] = rot.astype(o_ref.dtype)


def project(h, w_in, layer, col_blk, ncols, tn, tm, out_dtype, qk=None):
    nj = ncols // tn
    in_specs = [pl.BlockSpec((tm, D_MODEL), lambda j, t: (t, 0)),
                pl.BlockSpec((None, D_MODEL, tn), lambda j, t: (layer, 0, col_blk + j))]
    scratch = [pltpu.VMEM((D_MODEL, tn), BF16)]
    args = [h, w_in]
    if qk is None:
        body = _proj_plain_body
    else:
        gain, cos_t, sin_t = qk
        body = _proj_qk_body
        in_specs += [pl.BlockSpec((1, tn), lambda j, t: (0, j)),
                     pl.BlockSpec((tm, LANES), lambda j, t: (t, 0)),
                     pl.BlockSpec((tm, LANES), lambda j, t: (t, 0))]
        scratch.append(pltpu.VMEM((256, 256), BF16))
        args += [gain, cos_t, sin_t]
    return pl.pallas_call(
        body,
        grid=(nj, T // tm),
        in_specs=in_specs,
        out_specs=pl.BlockSpec((tm, tn), lambda j, t: (t, j)),
        out_shape=jax.ShapeDtypeStruct((T, ncols), out_dtype),
        scratch_shapes=scratch,
        compiler_params=_cp(2, 48),
        name="proj_qk" if qk is not None else "proj",
    )(*args)


def rope_tables():
    pos = jnp.concatenate([jnp.tile(jnp.arange(SEQ, dtype=I32), BATCH),
                           jnp.tile(PAST_LEN + jnp.arange(DEC_SEQ, dtype=I32), DEC_BATCH)])
    inv = ROPE_THETA ** (-jnp.arange(0, ROT_DIM, 2, dtype=F32) / ROT_DIM)
    ang = pos.astype(F32)[:, None] * inv[None, :]
    cos8, sin8 = jnp.cos(ang), jnp.sin(ang)
    rest = HEAD_DIM - ROT_DIM
    c64 = jnp.concatenate([cos8, cos8, jnp.ones((T, rest), F32)], axis=1)
    s64 = jnp.concatenate([-sin8, sin8, jnp.zeros((T, rest), F32)], axis=1)
    return jnp.tile(c64, (1, LANES // HEAD_DIM)), jnp.tile(s64, (1, LANES // HEAD_DIM))


def _lane_shift_variants(a):
    a0, a1 = a[:, :LANES], a[:, LANES:]
    r0 = pltpu.roll(a0, HEAD_DIM, axis=1)
    r1 = pltpu.roll(a1, HEAD_DIM, axis=1)
    low = lax.broadcasted_iota(I32, a0.shape, 1) < HEAD_DIM
    rot = jnp.concatenate([jnp.where(low, r1, r0), jnp.where(low, r0, r1)], axis=1)
    swap = lambda z: jnp.concatenate([z[:, LANES:], z[:, :LANES]], axis=1)
    return [a.astype(BF16), rot.astype(BF16), swap(a).astype(BF16), swap(rot).astype(BF16)]


def _attn_body(sinks_ref, q_ref, kh_ref, km_ref, vh_ref, vm_ref, o_ref, *, qrows, mask_first):
    i = pl.program_id(1)
    k_all = jnp.concatenate([kh_ref[...], km_ref[...]], axis=0)
    v_all = jnp.concatenate([vh_ref[...], vm_ref[...]], axis=0)
    n_groups = q_ref.shape[0] // qrows
    need = (n_groups - 1) * qrows + KEY_WIN
    if k_all.shape[0] < need:
        fill = jnp.zeros((need - k_all.shape[0], KV_W), F32)
        k_all = jnp.concatenate([k_all, fill], axis=0)
        v_all = jnp.concatenate([v_all, fill], axis=0)
    kvar = _lane_shift_variants(k_all)
    vvar = _lane_shift_variants(v_all)
    blk = lax.shift_right_logical(lax.broadcasted_iota(I32, (KEY_WIN, KV_W), 1), 6)
    zero = jnp.zeros((KEY_WIN, KV_W), BF16)
    q_chunk = lax.shift_right_logical(lax.broadcasted_iota(I32, (qrows, KEY_WIN), 0), 6)
    key = lax.broadcasted_iota(I32, (qrows, KEY_WIN), 1)
    band = (key >= q_chunk * CHUNK) & (key < q_chunk * CHUNK + WINDOW + CHUNK)
    for g in range(n_groups):
        w0 = g * qrows
        valid = band
        if mask_first and w0 < WINDOW:
            valid = band & (key >= jnp.where(i == 0, WINDOW - w0, 0))
        for x in range(N_KV_HEADS):
            kcat = jnp.concatenate([jnp.where(blk == j, kvar[(j - x) % GROUP][w0:w0 + KEY_WIN], zero)
                                    for j in range(GROUP)], axis=0)
            vcat = jnp.concatenate([jnp.where(blk == j, vvar[(j - x) % GROUP][w0:w0 + KEY_WIN], zero)
                                    for j in range(GROUP)], axis=0)
            qg = q_ref[w0:w0 + qrows, x * 256:(x + 1) * 256]
            s = lax.dot_general(qg, kcat, (((1,), (1,)), ((), ())),
                                preferred_element_type=F32) * ATTN_SCALE
            probs = []
            for j in range(GROUP):
                sj = jnp.where(valid, s[:, j * KEY_WIN:(j + 1) * KEY_WIN], NEG_INF)
                sink = sinks_ref[x * GROUP + j]
                m = jnp.maximum(jnp.max(sj, axis=1, keepdims=True), sink)
                p = jnp.exp(sj - m)
                den = jnp.sum(p, axis=1, keepdims=True) + jnp.exp(sink - m)
                probs.append((p / den).astype(BF16))
            og = _dot(jnp.concatenate(probs, axis=1), vcat)
            o_ref[w0:w0 + qrows, x * 256:(x + 1) * 256] = og.astype(o_ref.dtype)


def attention_prompt(sinks, q, k, v):
    rows = 4 * CHUNK
    steps = SEQ // rows
    hb = rows // WINDOW
    main = lambda b, i, s: (b * steps + i, 0)
    halo = lambda b, i, s: (jnp.maximum(b * steps * hb + i * hb - 1, b * steps * hb), 0)
    return pl.pallas_call(
        functools.partial(_attn_body, qrows=2 * CHUNK, mask_first=True),
        grid_spec=pltpu.PrefetchScalarGridSpec(
            num_scalar_prefetch=1,
            grid=(BATCH, steps),
            in_specs=[pl.BlockSpec((rows, ATTN_W), main),
                      pl.BlockSpec((WINDOW, KV_W), halo),
                      pl.BlockSpec((rows, KV_W), main),
                      pl.BlockSpec((WINDOW, KV_W), halo),
                      pl.BlockSpec((rows, KV_W), main)],
            out_specs=pl.BlockSpec((rows, ATTN_W), main)),
        out_shape=jax.ShapeDtypeStruct((TP, ATTN_W), BF16),
        compiler_params=_cp(2, 32),
        name="attn_prompt",
    )(sinks, q, k, k, v, v)


def attention_sample(sinks, q, k, v, cache_k, cache_v):
    first = TP // DEC_SEQ
    main = lambda b, i, s: (first + b, 0)
    halo = lambda b, i, s: (b, 0)
    return pl.pallas_call(
        functools.partial(_attn_body, qrows=CHUNK, mask_first=False),
        grid_spec=pltpu.PrefetchScalarGridSpec(
            num_scalar_prefetch=1,
            grid=(DEC_BATCH, 1),
            in_specs=[pl.BlockSpec((DEC_SEQ, ATTN_W), main),
                      pl.BlockSpec((WINDOW, KV_W), halo),
                      pl.BlockSpec((DEC_SEQ, KV_W), main),
                      pl.BlockSpec((WINDOW, KV_W), halo),
                      pl.BlockSpec((DEC_SEQ, KV_W), main)],
            out_specs=pl.BlockSpec((DEC_SEQ, ATTN_W), halo)),
        out_shape=jax.ShapeDtypeStruct((TS, ATTN_W), BF16),
        compiler_params=_cp(2, 32),
        name="attn_sample",
    )(sinks, q, cache_k, k, cache_v, v)


def _pool_body(u_ref, h_ref, pw_ref, sc_ref, y_ref, *, zero_first, full_count):
    i = pl.program_id(1)
    u = u_ref[...]
    tl = u.shape[0]
    halo = h_ref[...]
    if zero_first:
        halo = jnp.where(i == 0, 0.0, halo)
    ext = jnp.concatenate([halo, u], axis=0)
    pos = i * tl + lax.broadcasted_iota(I32, (tl, 1), 0)
    for g, w in enumerate(POOL_WINDOWS):
        sl = slice(g * POOL_GROUP_W, (g + 1) * POOL_GROUP_W)
        s = ext[:, sl]
        sh = 1
        while sh < w:
            s = s + pltpu.roll(s, sh, axis=0)
            sh *= 2
        tot = s[HALO:]
        cnt = float(w) if full_count else jnp.minimum(w, pos + 1).astype(F32)
        d = (tot / cnt - u[:, sl]).astype(BF16)
        yg = _dot(d, pw_ref[g].astype(BF16)) * sc_ref[:, sl]
        y_ref[:, sl] = yg.astype(y_ref.dtype)


def pool_prompt(u, pool_w, pool_scale, layer):
    tl = 512
    steps = SEQ // tl
    hpb = tl // HALO
    main = lambda b, i: (b * steps + i, 0)
    halo = lambda b, i: (jnp.maximum((b * steps + i) * hpb - 1, b * steps * hpb), 0)
    return pl.pallas_call(
        functools.partial(_pool_body, zero_first=True, full_count=False),
        grid=(BATCH, steps),
        in_specs=[pl.BlockSpec((tl, POOL_W), main),
                  pl.BlockSpec((HALO, POOL_W), halo),
                  pl.BlockSpec((None, 4, POOL_GROUP_W, POOL_GROUP_W), lambda b, i: (layer, 0, 0, 0)),
                  pl.BlockSpec((1, POOL_W), lambda b, i: (0, 0))],
        out_specs=pl.BlockSpec((tl, POOL_W), main),
        out_shape=jax.ShapeDtypeStruct((TP, POOL_W), BF16),
        compiler_params=_cp(2, 32),
        name="pool_prompt",
    )(u, u, pool_w, pool_scale)


def pool_sample(u, hist, pool_w, pool_scale, layer):
    first = TP // DEC_SEQ
    return pl.pallas_call(
        functools.partial(_pool_body, zero_first=False, full_count=True),
        grid=(DEC_BATCH, 1),
        in_specs=[pl.BlockSpec((DEC_SEQ, POOL_W), lambda b, i: (first + b, 0)),
                  pl.BlockSpec((HALO, POOL_W), lambda b, i: (b, 0)),
                  pl.BlockSpec((None, 4, POOL_GROUP_W, POOL_GROUP_W), lambda b, i: (layer, 0, 0, 0)),
                  pl.BlockSpec((1, POOL_W), lambda b, i: (0, 0))],
        out_specs=pl.BlockSpec((DEC_SEQ, POOL_W), lambda b, i: (b, 0)),
        out_shape=jax.ShapeDtypeStruct((TS, POOL_W), BF16),
        compiler_params=_cp(2, 32),
        name="pool_sample",
    )(u, hist, pool_w, pool_scale)


def _branch_body(h_ref, op_ref, os_ref, yp_ref, ys_ref, wga_ref, wgb_ref, wa_ref, wp_ref, out_ref,
                 wga, wgb, wa, wp):
    t = pl.program_id(1)

    @pl.when(t == 0)
    def _():
        wga[...] = wga_ref[...].astype(BF16)
        wgb[...] = wgb_ref[...].astype(BF16)
        wa[...] = wa_ref[...].astype(BF16)
        wp[...] = wp_ref[...].astype(BF16)

    h = h_ref[...]
    ga = jax.nn.sigmoid(_dot(h, wga[...]))
    gb = jax.nn.sigmoid(_dot(h, wgb[...]))
    a = _dot(_load_rows((op_ref, os_ref), t), wa[...])
    c = _dot(_load_rows((yp_ref, ys_ref), t), wp[...])
    out_ref[...] = (ga * a + gb * c).astype(out_ref.dtype)


def branch_mix(h, o_p, o_s, y_p, y_s, w_in, w_attn_br, w_pool_br, layer):
    tn, tm = 512, 512
    ga_blk = (ATTN_W + 2 * KV_W + POOL_W) // tn
    gb_blk = ga_blk + D_MODEL // tn
    return pl.pallas_call(
        _branch_body,
        grid=(D_MODEL // tn, T // tm),
        in_specs=([pl.BlockSpec((tm, D_MODEL), lambda j, t: (t, 0))]
                  + _row_specs(2, tm, ATTN_W, 2) + _row_specs(2, tm, POOL_W, 2)
                  + [pl.BlockSpec((None, D_MODEL, tn), lambda j, t: (layer, 0, ga_blk + j)),
                     pl.BlockSpec((None, D_MODEL, tn), lambda j, t: (layer, 0, gb_blk + j)),
                     pl.BlockSpec((None, ATTN_W, tn), lambda j, t: (layer, 0, j)),
                     pl.BlockSpec((None, POOL_W, tn), lambda j, t: (layer, 0, j))]),
        out_specs=pl.BlockSpec((tm, tn), lambda j, t: (t, j)),
        out_shape=jax.ShapeDtypeStruct((T, D_MODEL), BF16),
        scratch_shapes=[pltpu.VMEM((D_MODEL, tn), BF16), pltpu.VMEM((D_MODEL, tn), BF16),
                        pltpu.VMEM((ATTN_W, tn), BF16), pltpu.VMEM((POOL_W, tn), BF16)],
        compiler_params=_cp(2, 56),
        name="branch_mix",
    )(h, o_p, o_s, y_p, y_s, w_in, w_in, w_attn_br, w_pool_br)


def _wout_body(*refs, n_x, router):
    m_ref, *x_refs = refs[:1 + n_x]
    rest = refs[1 + n_x:]
    if router:
        w_ref, g_ref, r_ref, x1_ref, h2_ref, lg_ref, wbf = rest
    else:
        w_ref, g_ref, x1_ref, h2_ref, wbf = rest
    t = pl.program_id(0)

    @pl.when(t == 0)
    def _():
        wbf[...] = w_ref[...].astype(BF16)

    x1 = _load_rows(x_refs, t) + _dot(m_ref[...], wbf[...])
    x1_ref[...] = x1
    h2 = _rms(x1, g_ref[...])
    h2_ref[...] = h2.astype(h2_ref.dtype)
    if router:
        r = r_ref[...]
        r_hi = r.astype(BF16)
        r_lo = (r - r_hi.astype(F32)).astype(BF16)
        h_hi = h2.astype(BF16)
        h_lo = (h2 - h_hi.astype(F32)).astype(BF16)
        lg_ref[...] = _dot(h_hi, r_hi) + (_dot(h_lo, r_hi) + _dot(h_hi, r_lo))


def out_proj(mixed, x_parts, w_out, g, layer, router_w=None):
    tm = 256
    router = router_w is not None
    row = lambda t: (t, 0)
    const2 = lambda t: (0, 0)
    in_specs = ([pl.BlockSpec((tm, D_MODEL), row)] + _row_specs(len(x_parts), tm, D_MODEL, 1)
                + [pl.BlockSpec((None, D_MODEL, D_MODEL), lambda t: (layer, 0, 0),
                                pipeline_mode=pl.Buffered(1)),
                   pl.BlockSpec((1, D_MODEL), const2)])
    out_specs = [pl.BlockSpec((tm, D_MODEL), row), pl.BlockSpec((tm, D_MODEL), row)]
    out_shape = [jax.ShapeDtypeStruct((T, D_MODEL), F32),
                 jax.ShapeDtypeStruct((T, D_MODEL), F32 if router else BF16)]
    args = [mixed, *x_parts, w_out, g.reshape(1, D_MODEL)]
    if router:
        in_specs.append(pl.BlockSpec((D_MODEL, LANES), const2))
        out_specs.append(pl.BlockSpec((tm, LANES), row))
        out_shape.append(jax.ShapeDtypeStruct((T, LANES), F32))
        args.append(router_w)
    return pl.pallas_call(
        functools.partial(_wout_body, n_x=len(x_parts), router=router),
        grid=(T // tm,),
        in_specs=in_specs,
        out_specs=out_specs,
        out_shape=out_shape,
        scratch_shapes=[pltpu.VMEM((D_MODEL, D_MODEL), BF16)],
        compiler_params=_cp(1, 56),
        name="out_proj",
    )(*args)


def _ffn_up_body(h_ref, wg_ref, wu_ref, o_ref, wg, wu):
    @pl.when(pl.program_id(1) == 0)
    def _():
        wg[...] = wg_ref[...].astype(BF16)
        wu[...] = wu_ref[...].astype(BF16)

    h = h_ref[...]
    g = _dot(h, wg[...])
    o_ref[...] = (jax.nn.silu(g) * _dot(h, wu[...])).astype(o_ref.dtype)


def ffn_up(h2, w_gate_up, idx):
    tm = 1024
    return pl.pallas_call(
        _ffn_up_body,
        grid=(N_FF_BLK, T // tm),
        in_specs=[pl.BlockSpec((tm, D_MODEL), lambda j, t: (t, 0)),
                  pl.BlockSpec((None, D_MODEL, FF_BLK), lambda j, t: (idx, 0, j)),
                  pl.BlockSpec((None, D_MODEL, FF_BLK), lambda j, t: (idx, 0, N_FF_BLK + j))],
        out_specs=pl.BlockSpec((tm, FF_BLK), lambda j, t: (t, j)),
        out_shape=jax.ShapeDtypeStruct((T, D_FF), BF16),
        scratch_shapes=[pltpu.VMEM((D_MODEL, FF_BLK), BF16), pltpu.VMEM((D_MODEL, FF_BLK), BF16)],
        compiler_params=_cp(2, 48),
        name="ffn_up",
    )(h2, w_gate_up, w_gate_up)


def _ffn_down_body(a_ref, w_ref, x_ref, o_ref, wbf):
    @pl.when(pl.program_id(1) == 0)
    def _():
        wbf[...] = w_ref[...].astype(BF16)

    o_ref[...] = x_ref[...] + _dot(a_ref[...], wbf[...])


def ffn_down(act, w_down, idx, x1):
    tm, tn = 512, 512
    return pl.pallas_call(
        _ffn_down_body,
        grid=(D_MODEL // tn, T // tm),
        in_specs=[pl.BlockSpec((tm, D_FF), lambda j, t: (t, 0)),
                  pl.BlockSpec((None, D_FF, tn), lambda j, t: (idx, 0, j)),
                  pl.BlockSpec((tm, tn), lambda j, t: (t, j))],
        out_specs=pl.BlockSpec((tm, tn), lambda j, t: (t, j)),
        out_shape=jax.ShapeDtypeStruct((T, D_MODEL), F32),
        scratch_shapes=[pltpu.VMEM((D_FF, tn), BF16)],
        compiler_params=_cp(2, 56),
        name="ffn_down",
    )(act, w_down, x1)


def _ple_body(h_ref, pp_ref, ps_ref, x_ref, wg_ref, wp_ref, *rest, last):
    t = pl.program_id(0)
    if last:
        yp_ref, ys_ref, wg, wp = rest
    else:
        gn_ref, x3_ref, hn_ref, wg, wp = rest

    @pl.when(t == 0)
    def _():
        wg[...] = wg_ref[...].astype(BF16)
        wp[...] = wp_ref[...].astype(BF16)

    gate = jax.nn.sigmoid(_dot(h_ref[...], wg[...]))
    p = _load_rows((pp_ref, ps_ref), t).astype(BF16)
    x3 = x_ref[...] + _dot(p, wp[...]) * gate
    if last:
        pb = TP // x3.shape[0]

        @pl.when(t < pb)
        def _():
            yp_ref[...] = x3

        @pl.when(t >= pb)
        def _():
            ys_ref[...] = x3
    else:
        x3_ref[...] = x3
        hn_ref[...] = _rms(x3, gn_ref[...]).astype(hn_ref.dtype)


def ple(h3, p_prompt, p_sample, x2, w_ple_gate, w_ple, layer, g_next):
    tm = 256
    last = g_next is None
    pb = TP // tm
    row = lambda t: (t, 0)
    prow = lambda t: (jnp.minimum(t, pb - 1), 0)
    srow = lambda t: (jnp.maximum(t - pb, 0), 0)
    in_specs = [pl.BlockSpec((tm, D_MODEL), row),
                pl.BlockSpec((None, tm, PLE_DIM), lambda t: (layer, jnp.minimum(t, pb - 1), 0)),
                pl.BlockSpec((None, tm, PLE_DIM), lambda t: (layer, jnp.maximum(t - pb, 0), 0)),
                pl.BlockSpec((tm, D_MODEL), row),
                pl.BlockSpec((None, D_MODEL, D_MODEL), lambda t: (layer, 0, 0), pipeline_mode=pl.Buffered(1)),
                pl.BlockSpec((None, PLE_DIM, D_MODEL), lambda t: (layer, 0, 0), pipeline_mode=pl.Buffered(1))]
    args = [h3, p_prompt.reshape(DEPTH, TP, PLE_DIM), p_sample.reshape(DEPTH, TS, PLE_DIM), x2, w_ple_gate, w_ple]
    if last:
        out_specs = [pl.BlockSpec((tm, D_MODEL), prow), pl.BlockSpec((tm, D_MODEL), srow)]
        out_shape = [jax.ShapeDtypeStruct((TP, D_MODEL), F32), jax.ShapeDtypeStruct((TS, D_MODEL), F32)]
    else:
        in_specs.append(pl.BlockSpec((1, D_MODEL), lambda t: (0, 0)))
        args.append(g_next.reshape(1, D_MODEL))
        out_specs = [pl.BlockSpec((tm, D_MODEL), row), pl.BlockSpec((tm, D_MODEL), row)]
        out_shape = [jax.ShapeDtypeStruct((T, D_MODEL), F32), jax.ShapeDtypeStruct((T, D_MODEL), BF16)]
    return pl.pallas_call(
        functools.partial(_ple_body, last=last),
        grid=(T // tm,),
        in_specs=in_specs,
        out_specs=out_specs,
        out_shape=out_shape,
        scratch_shapes=[pltpu.VMEM((D_MODEL, D_MODEL), BF16), pltpu.VMEM((PLE_DIM, D_MODEL), BF16)],
        compiler_params=_cp(1, 56),
        name="ple",
    )(*args)


def _router_body(lg_ref, meta_ref, cnt_ref, carry):
    @pl.when(pl.program_id(0) == 0)
    def _():
        carry[...] = jnp.zeros_like(carry)

    lg = lg_ref[...]
    tm = lg.shape[0]
    lane = lax.broadcasted_iota(I32, (tm, LANES), 1)
    lane_f = lane.astype(F32)
    big = float(LANES)
    l1 = jnp.where(lane < N_EXPERTS, lg, -jnp.inf)
    m1 = jnp.max(l1, axis=1, keepdims=True)
    i1 = jnp.min(jnp.where(l1 == m1, lane_f, big), axis=1, keepdims=True)
    oh1 = lane_f == i1
    l2 = jnp.where(oh1, -jnp.inf, l1)
    m2 = jnp.max(l2, axis=1, keepdims=True)
    i2 = jnp.min(jnp.where(l2 == m2, lane_f, big), axis=1, keepdims=True)
    oh2 = lane_f == i2
    e = jnp.exp(m2 - m1)
    w1 = 1.0 / (1.0 + e)
    w2 = e / (1.0 + e)
    oh = jnp.where(oh1 | oh2, 1.0, 0.0)
    r = lax.broadcasted_iota(I32, (tm, tm), 0)
    c = lax.broadcasted_iota(I32, (tm, tm), 1)
    tri = jnp.where(c < r, 1.0, 0.0).astype(BF16)
    before = _dot(tri, oh.astype(BF16)) + carry[0:1, :]
    r1 = jnp.sum(jnp.where(oh1, before, 0.0), axis=1, keepdims=True)
    r2 = jnp.sum(jnp.where(oh2, before, 0.0), axis=1, keepdims=True)
    meta = jnp.where(lane == 0, i1, jnp.where(lane == 1, i2, jnp.where(lane == 2, w1,
           jnp.where(lane == 3, w2, jnp.where(lane == 4, r1, jnp.where(lane == 5, r2, 0.0))))))
    meta_ref[...] = meta
    total = carry[0:1, :] + jnp.sum(oh, axis=0, keepdims=True)
    carry[...] = jnp.broadcast_to(total, carry.shape)
    cnt_ref[...] = jnp.broadcast_to(total, cnt_ref.shape)


def route(logits):
    tm = 512
    return pl.pallas_call(
        _router_body,
        grid=(T // tm,),
        in_specs=[pl.BlockSpec((tm, LANES), lambda t: (t, 0))],
        out_specs=[pl.BlockSpec((tm, LANES), lambda t: (t, 0)),
                   pl.BlockSpec((8, LANES), lambda t: (0, 0))],
        out_shape=[jax.ShapeDtypeStruct((T, LANES), F32), jax.ShapeDtypeStruct((8, LANES), F32)],
        scratch_shapes=[pltpu.VMEM((8, LANES), F32)],
        compiler_params=_cp(1, 32),
        name="moe_route",
    )(logits)


def moe_plan(meta, cnt_rows):
    i1 = meta[:, 0].astype(I32)
    i2 = meta[:, 1].astype(I32)
    r1 = meta[:, 4].astype(I32)
    r2 = meta[:, 5].astype(I32)
    cnt = cnt_rows[0, :N_EXPERTS].astype(I32)
    tiles_e = (cnt + MOE_TILE - 1) // MOE_TILE
    cum_tiles = jnp.cumsum(tiles_e)
    pad_off = (cum_tiles - tiles_e) * MOE_TILE
    n_tiles = cum_tiles[-1:]
    e_ids = jnp.arange(N_EXPERTS, dtype=I32)
    pick = lambda idx: jnp.sum(jnp.where(idx[:, None] == e_ids[None, :], pad_off[None, :], 0), axis=1)
    pos1 = pick(i1) + r1
    pos2 = pick(i2) + r2
    last_e = jnp.max(jnp.where(cnt > 0, e_ids, 0))
    tile_ids = jnp.arange(MOE_TILES, dtype=I32)
    tile_e = jnp.minimum(jnp.sum(tile_ids[:, None] >= cum_tiles[None, :], axis=1).astype(I32), last_e)
    pad_rows = (pad_off + cnt)[:, None] + jnp.arange(MOE_TILE - 1, dtype=I32)[None, :]
    pad_keys = jnp.where(pad_rows < (pad_off + tiles_e * MOE_TILE)[:, None], pad_rows, 2 * MOE_ROWS).reshape(-1)
    tok = jnp.arange(T, dtype=I32)
    keys = jnp.concatenate([pos1, pos2, pad_keys])
    vals = jnp.concatenate([tok, tok, jnp.zeros((MOE_PAD,), I32)])
    row_tok = lax.sort((keys, vals), num_keys=1)[1][:MOE_ROWS]
    return pos1, pos2, tile_e, n_tiles, row_tok


def _row_copy(src_hbm, dst_vmem, sem, src_row, dst_row):
    return pltpu.make_async_copy(src_hbm.at[pl.ds(src_row, 1)], dst_vmem.at[pl.ds(dst_row, 1)], sem)


def _start_rows(idx_ref, src_hbm, dst_vmem, sem):
    def body(r2, carry):
        for prio in range(2):
            r = 2 * r2 + prio
            _row_copy(src_hbm, dst_vmem, sem, idx_ref[0, 0, r], r).start(priority=prio)
        return carry

    lax.fori_loop(0, dst_vmem.shape[0] // 2, body, 0, unroll=4)


def _wait_rows(src_hbm, dst_vmem, sem):
    def body(r, carry):
        _row_copy(src_hbm, dst_vmem, sem, 0, r).wait()
        return carry

    lax.fori_loop(0, dst_vmem.shape[0], body, 0, unroll=8)


def _zero_dead_tile(nt_ref, r, o_ref):
    @pl.when(r >= nt_ref[0])
    def _():
        o_ref[...] = jnp.zeros_like(o_ref)


def _dispatch_body(nt_ref, tok_ref, nxt_ref, h_hbm, o_ref, buf, sem):
    r = pl.program_id(0)
    slot = r % 2

    @pl.when(r == 0)
    def _():
        _start_rows(tok_ref, h_hbm, buf.at[0], sem.at[0])

    @pl.when(r + 1 < nt_ref[0])
    def _():
        _start_rows(nxt_ref, h_hbm, buf.at[1 - slot], sem.at[1 - slot])

    @pl.when(r < nt_ref[0])
    def _():
        _wait_rows(h_hbm, buf.at[slot], sem.at[slot])
        o_ref[...] = buf[slot].astype(o_ref.dtype)

    _zero_dead_tile(nt_ref, r, o_ref)


def moe_dispatch(n_tiles, row_tok, h2f):
    smem = lambda off: pl.BlockSpec((1, 1, MOE_TILE),
                                    lambda r, nt: (jnp.minimum(r + off, MOE_TILES - 1), 0, 0),
                                    memory_space=pltpu.SMEM)
    tok3 = row_tok.reshape(MOE_TILES, 1, MOE_TILE)
    return pl.pallas_call(
        _dispatch_body,
        grid_spec=pltpu.PrefetchScalarGridSpec(
            num_scalar_prefetch=1,
            grid=(MOE_TILES,),
            in_specs=[smem(0), smem(1), pl.BlockSpec(memory_space=pl.ANY)],
            out_specs=pl.BlockSpec((MOE_TILE, D_MODEL), lambda r, nt: (r, 0)),
            scratch_shapes=[pltpu.VMEM((2, MOE_TILE, D_MODEL), F32), pltpu.SemaphoreType.DMA((2,))]),
        out_shape=jax.ShapeDtypeStruct((MOE_ROWS, D_MODEL), BF16),
        compiler_params=_cp(1, 32),
        name="moe_dispatch",
    )(n_tiles, tok3, tok3, h2f)


def _expert_changed(te_ref, r):
    return jnp.logical_or(r == 0, te_ref[r] != te_ref[jnp.maximum(r - 1, 0)])


def _moe_up_body(te_ref, nt_ref, a_ref, wg_ref, wu_ref, o_ref, wg, wu):
    r = pl.program_id(1)

    @pl.when(r < nt_ref[0])
    def _():
        @pl.when(_expert_changed(te_ref, r))
        def _():
            wg[...] = wg_ref[...].astype(BF16)
            wu[...] = wu_ref[...].astype(BF16)

        a = a_ref[...]
        g = _dot(a, wg[...])
        o_ref[...] = (jax.nn.silu(g) * _dot(a, wu[...])).astype(o_ref.dtype)

    _zero_dead_tile(nt_ref, r, o_ref)


def moe_up(tile_e, n_tiles, xs, w_gate_up, idx):
    live = lambda j, r, te, nt: (jnp.minimum(r, nt[0] - 1), 0)
    return pl.pallas_call(
        _moe_up_body,
        grid_spec=pltpu.PrefetchScalarGridSpec(
            num_scalar_prefetch=2,
            grid=(N_FF_BLK, MOE_TILES),
            in_specs=[pl.BlockSpec((MOE_TILE, D_MODEL), live),
                      pl.BlockSpec((None, None, D_MODEL, FF_BLK),
                                   lambda j, r, te, nt: (idx, te[r], 0, j)),
                      pl.BlockSpec((None, None, D_MODEL, FF_BLK),
                                   lambda j, r, te, nt: (idx, te[r], 0, N_FF_BLK + j))],
            out_specs=pl.BlockSpec((MOE_TILE, FF_BLK), lambda j, r, te, nt: (r, j)),
            scratch_shapes=[pltpu.VMEM((D_MODEL, FF_BLK), BF16), pltpu.VMEM((D_MODEL, FF_BLK), BF16)]),
        out_shape=jax.ShapeDtypeStruct((MOE_ROWS, D_FF), BF16),
        compiler_params=_cp(2, 48),
        name="moe_up",
    )(tile_e, n_tiles, xs, w_gate_up, w_gate_up)


def _moe_down_body(te_ref, nt_ref, a_ref, w_ref, o_ref, wbf):
    r = pl.program_id(1)

    @pl.when(r < nt_ref[0])
    def _():
        @pl.when(_expert_changed(te_ref, r))
        def _():
            wbf[...] = w_ref[...].astype(BF16)

        o_ref[...] = _dot(a_ref[...], wbf[...])

    _zero_dead_tile(nt_ref, r, o_ref)


def moe_down(tile_e, n_tiles, act, w_down, idx):
    tn = 512
    return pl.pallas_call(
        _moe_down_body,
        grid_spec=pltpu.PrefetchScalarGridSpec(
            num_scalar_prefetch=2,
            grid=(D_MODEL // tn, MOE_TILES),
            in_specs=[pl.BlockSpec((MOE_TILE, D_FF), lambda j, r, te, nt: (jnp.minimum(r, nt[0] - 1), 0)),
                      pl.BlockSpec((None, None, D_FF, tn), lambda j, r, te, nt: (idx, te[r], 0, j))],
            out_specs=pl.BlockSpec((MOE_TILE, tn), lambda j, r, te, nt: (r, j)),
            scratch_shapes=[pltpu.VMEM((D_FF, tn), BF16)]),
        out_shape=jax.ShapeDtypeStruct((MOE_ROWS, D_MODEL), F32),
        compiler_params=_cp(2, 56),
        name="moe_down",
    )(tile_e, n_tiles, act, w_down)


def _combine_body(p1_ref, p2_ref, n1_ref, n2_ref, ys_hbm, meta_ref, x_ref, g_ref, x2_ref, h3_ref,
                  b1, b2, sem1, sem2):
    t = pl.program_id(0)
    slot = t % 2

    @pl.when(t == 0)
    def _():
        _start_rows(p1_ref, ys_hbm, b1.at[0], sem1.at[0])
        _start_rows(p2_ref, ys_hbm, b2.at[0], sem2.at[0])

    @pl.when(t + 1 < pl.num_programs(0))
    def _():
        _start_rows(n1_ref, ys_hbm, b1.at[1 - slot], sem1.at[1 - slot])
        _start_rows(n2_ref, ys_hbm, b2.at[1 - slot], sem2.at[1 - slot])

    _wait_rows(ys_hbm, b1.at[slot], sem1.at[slot])
    _wait_rows(ys_hbm, b2.at[slot], sem2.at[slot])
    meta = meta_ref[...]
    w1 = meta[:, 2:3]
    w2 = meta[:, 3:4]
    x2 = x_ref[...] + (w1 * b1[slot] + w2 * b2[slot])
    x2_ref[...] = x2
    h3_ref[...] = _rms(x2, g_ref[...]).astype(h3_ref.dtype)


def moe_combine(pos1, pos2, ys, meta, x1, g):
    tm = 256
    nblk = T // tm
    row = lambda t: (t, 0)
    smem = lambda off: pl.BlockSpec((1, 1, tm), lambda t: (jnp.minimum(t + off, nblk - 1), 0, 0),
                                    memory_space=pltpu.SMEM)
    p1 = pos1.reshape(nblk, 1, tm)
    p2 = pos2.reshape(nblk, 1, tm)
    return pl.pallas_call(
        _combine_body,
        grid=(nblk,),
        in_specs=[smem(0), smem(0), smem(1), smem(1),
                  pl.BlockSpec(memory_space=pl.ANY),
                  pl.BlockSpec((tm, LANES), row),
                  pl.BlockSpec((tm, D_MODEL), row),
                  pl.BlockSpec((1, D_MODEL), lambda t: (0, 0))],
        out_specs=[pl.BlockSpec((tm, D_MODEL), row), pl.BlockSpec((tm, D_MODEL), row)],
        out_shape=[jax.ShapeDtypeStruct((T, D_MODEL), F32), jax.ShapeDtypeStruct((T, D_MODEL), BF16)],
        scratch_shapes=[pltpu.VMEM((2, tm, D_MODEL), F32), pltpu.VMEM((2, tm, D_MODEL), F32),
                        pltpu.SemaphoreType.DMA((2,)), pltpu.SemaphoreType.DMA((2,))],
        compiler_params=_cp(1, 48),
        name="moe_combine",
    )(p1, p2, p1, p2, ys, meta, x1, g.reshape(1, D_MODEL))


def _last_rows(a, batch, seq, n):
    return a.reshape(batch, seq, a.shape[-1])[:, seq - n:]


def kernel(x_prompt, x_sample, cache_k, cache_v, state_pool, p_prompt, p_sample, norm_mix, w_in, q_norm, k_norm, sinks, pool_w, pool_scale, w_attn_br, w_pool_br, w_out, norm_ffn, ffn_w_gate_up, ffn_w_down, moe_router, moe_w_gate_up, moe_w_down, norm_ple, w_ple, w_ple_gate):
    x_parts = (x_prompt.reshape(TP, D_MODEL), x_sample.reshape(TS, D_MODEL))
    cos_t, sin_t = rope_tables()
    outs = {k: [] for k in ("kp", "vp", "pp", "ks", "vs", "ps")}
    h = rmsnorm_bf16(x_parts, norm_mix[0])

    for l in range(DEPTH):
        q_gain = jnp.tile(q_norm[l], ATTN_W // HEAD_DIM).reshape(1, ATTN_W)
        k_gain = jnp.tile(k_norm[l], KV_W // HEAD_DIM).reshape(1, KV_W)
        q = project(h, w_in, l, 0, ATTN_W, 512, 512, BF16, qk=(q_gain, cos_t, sin_t))
        k = project(h, w_in, l, ATTN_W // KV_W, KV_W, KV_W, 512, F32, qk=(k_gain, cos_t, sin_t))
        v = project(h, w_in, l, (ATTN_W + KV_W) // KV_W, KV_W, KV_W, 1024, F32)
        u = project(h, w_in, l, (ATTN_W + 2 * KV_W) // 512, POOL_W, 512, 1024, F32)

        ck = cache_k[l].reshape(DEC_BATCH * WINDOW, KV_W)
        cv = cache_v[l].reshape(DEC_BATCH * WINDOW, KV_W)
        o_p = attention_prompt(sinks[l], q, k, v)
        o_s = attention_sample(sinks[l], q, k, v, ck, cv)

        hist = jnp.pad(state_pool[l], ((0, 0), (HALO - POOL_HIST, 0), (0, 0))).reshape(DEC_BATCH * HALO, POOL_W)
        scale = pool_scale[l].reshape(1, POOL_W)
        y_p = pool_prompt(u, pool_w, scale, l)
        y_s = pool_sample(u, hist, pool_w, scale, l)

        mixed = branch_mix(h, o_p, o_s, y_p, y_s, w_in, w_attn_br, w_pool_br, l)

        if l % 2 == 0:
            x1, h2 = out_proj(mixed, x_parts, w_out, norm_ffn[l], l)
            act = ffn_up(h2, ffn_w_gate_up, l // 2)
            x2 = ffn_down(act, ffn_w_down, l // 2, x1)
            h3 = rmsnorm_bf16((x2,), norm_ple[l])
        else:
            router_w = jnp.pad(moe_router[l // 2], ((0, 0), (0, LANES - N_EXPERTS)))
            x1, h2f, logits = out_proj(mixed, x_parts, w_out, norm_ffn[l], l, router_w=router_w)
            meta, cnt_rows = route(logits)
            pos1, pos2, tile_e, n_tiles, row_tok = moe_plan(meta, cnt_rows)
            xs = moe_dispatch(n_tiles, row_tok, h2f)
            act = moe_up(tile_e, n_tiles, xs, moe_w_gate_up, l // 2)
            ys = moe_down(tile_e, n_tiles, act, moe_w_down, l // 2)
            x2, h3 = moe_combine(pos1, pos2, ys, meta, x1, norm_ple[l])

        if l + 1 < DEPTH:
            x, h = ple(h3, p_prompt, p_sample, x2, w_ple_gate, w_ple, l, norm_mix[l + 1])
            x_parts = (x,)
        else:
            y_prompt, y_sample = ple(h3, p_prompt, p_sample, x2, w_ple_gate, w_ple, l, None)

        kv_shape = (N_KV_HEADS, HEAD_DIM)
        outs["kp"].append(_last_rows(k[:TP], BATCH, SEQ, WINDOW).reshape(BATCH, WINDOW, *kv_shape))
        outs["vp"].append(_last_rows(v[:TP], BATCH, SEQ, WINDOW).reshape(BATCH, WINDOW, *kv_shape))
        ksn = k[TP:].reshape(DEC_BATCH, DEC_SEQ, *kv_shape)
        vsn = v[TP:].reshape(DEC_BATCH, DEC_SEQ, *kv_shape)
        outs["ks"].append(jnp.concatenate([cache_k[l][:, DEC_SEQ:], ksn], axis=1))
        outs["vs"].append(jnp.concatenate([cache_v[l][:, DEC_SEQ:], vsn], axis=1))
        outs["pp"].append(_last_rows(u[:TP], BATCH, SEQ, POOL_HIST))
        us = u[TP:].reshape(DEC_BATCH, DEC_SEQ, POOL_W)
        outs["ps"].append(jnp.concatenate([state_pool[l], us], axis=1)[:, -POOL_HIST:])

    return (y_prompt.reshape(BATCH, SEQ, D_MODEL), y_sample.reshape(DEC_BATCH, DEC_SEQ, D_MODEL),
            jnp.stack(outs["kp"]), jnp.stack(outs["vp"]), jnp.stack(outs["pp"]),
            jnp.stack(outs["ks"]), jnp.stack(outs["vs"]), jnp.stack(outs["ps"]))
```

```python
import functools

import jax
import jax.numpy as jnp
from jax import lax
from jax.experimental import pallas as pl
from jax.experimental.pallas import tpu as pltpu

F32 = jnp.float32
BF16 = jnp.bfloat16
I32 = jnp.int32

D_MODEL = 2048
BATCH = 2
SEQ = 4096
DEPTH = 2
DEC_BATCH = 16
DEC_SEQ = 64
PAST_LEN = 2048
CHUNK = 64
N_HEADS = 16
N_KV_HEADS = 4
HEAD_DIM = 64
GROUP = N_HEADS // N_KV_HEADS
WINDOW = 128
ROT_DIM = HEAD_DIM // 4
ROPE_THETA = 500000.0
ATTN_SCALE = HEAD_DIM ** -0.5
POOL_WINDOWS = (2, 4, 8, 16)
POOL_W = 1024
POOL_GROUP_W = 256
POOL_HIST = 15
ATTN_W = N_HEADS * HEAD_DIM
KV_W = N_KV_HEADS * HEAD_DIM
IN_W = ATTN_W + 2 * KV_W + POOL_W + 2 * D_MODEL
D_FF = 5632
N_EXPERTS = 8
PLE_DIM = 256
RMS_EPS = 1e-6
NEG_INF = -1e30
LOG2E = 1.4426950408889634

TP = BATCH * SEQ
TS = DEC_BATCH * DEC_SEQ
T = TP + TS

LANES = 128
HALO = 16
KEY_WIN = 256
MOE_TILE = 512
MOE_PAD = N_EXPERTS * (MOE_TILE - 1)
MOE_TILES = (2 * T + MOE_PAD) // MOE_TILE
MOE_ROWS = MOE_TILES * MOE_TILE
GATHER_ROWS = 256
FF_BLK = 512
N_FF_BLK = D_FF // FF_BLK


def _cp(n_axes, vmem_mb):
    return pltpu.CompilerParams(dimension_semantics=("arbitrary",) * n_axes,
                                vmem_limit_bytes=vmem_mb * 2 ** 20)


def _dot(a, b):
    return jnp.dot(a, b, preferred_element_type=F32)


def _rms(x, g):
    ms = jnp.mean(x * x, axis=-1, keepdims=True)
    return (x * lax.rsqrt(ms + RMS_EPS)) * g


def _row_specs(n_parts, tm, width, nd):
    if n_parts == 1:
        return [pl.BlockSpec((tm, width), lambda *g: (g[nd - 1], 0))]
    pb = TP // tm
    return [pl.BlockSpec((tm, width), lambda *g: (jnp.minimum(g[nd - 1], pb - 1), 0)),
            pl.BlockSpec((tm, width), lambda *g: (jnp.maximum(g[nd - 1] - pb, 0), 0))]


def _load_rows(refs, t):
    if len(refs) == 1:
        return refs[0][...]
    tm = refs[0].shape[0]
    return jnp.where(t < TP // tm, refs[0][...], refs[1][...])


def _rmsnorm_body(*refs):
    *x_refs, g_ref, o_ref = refs
    o_ref[...] = _rms(_load_rows(x_refs, pl.program_id(0)), g_ref[...]).astype(o_ref.dtype)


def rmsnorm_bf16(x_parts, g):
    tm = 512
    return pl.pallas_call(
        _rmsnorm_body,
        grid=(T // tm,),
        in_specs=_row_specs(len(x_parts), tm, D_MODEL, 1) + [pl.BlockSpec((1, D_MODEL), lambda i: (0, 0))],
        out_specs=pl.BlockSpec((tm, D_MODEL), lambda i: (i, 0)),
        out_shape=jax.ShapeDtypeStruct((T, D_MODEL), BF16),
        compiler_params=_cp(1, 32),
        name="rmsnorm",
    )(*x_parts, g.reshape(1, D_MODEL))


def _proj_plain_body(a_ref, w_ref, o_ref, wbf):
    @pl.when(pl.program_id(1) == 0)
    def _():
        wbf[...] = w_ref[...].astype(BF16)

    o_ref[...] = _dot(a_ref[...], wbf[...]).astype(o_ref.dtype)


def _proj_qk_body(a_ref, w_ref, gain_ref, cos_ref, sin_ref, o_ref, wbf, seg):
    @pl.when(pl.program_id(1) == 0)
    def _():
        wbf[...] = w_ref[...].astype(BF16)
        r = lax.shift_right_logical(lax.broadcasted_iota(I32, seg.shape, 0), 6)
        c = lax.shift_right_logical(lax.broadcasted_iota(I32, seg.shape, 1), 6)
        seg[...] = jnp.where(r == c, 1.0, 0.0).astype(BF16)

    acc = _dot(a_ref[...], wbf[...])
    tm, tn = acc.shape
    sq = acc * acc
    hi = sq.astype(BF16)
    lo = (sq - hi.astype(F32)).astype(BF16)
    gain = gain_ref[...]
    cos = cos_ref[...]
    sin = sin_ref[...]
    first = (lax.broadcasted_iota(I32, (tm, LANES), 1) & (HEAD_DIM - 1)) < ROT_DIM // 2
    for cc in range(tn // 256):
        sl = slice(cc * 256, (cc + 1) * 256)
        ssum = _dot(hi[:, sl], seg[...]) + _dot(lo[:, sl], seg[...])
        y = (acc[:, sl] * lax.rsqrt(ssum * (1.0 / HEAD_DIM) + RMS_EPS)) * gain[:, sl]
        for hh in range(256 // LANES):
            yc = y[:, hh * LANES:(hh + 1) * LANES]
            up = pltpu.roll(yc, LANES - ROT_DIM // 2, axis=1)
            dn = pltpu.roll(yc, ROT_DIM // 2, axis=1)
            rot = yc * cos + jnp.where(first, up, dn) * sin
            lo_c = cc * 256 + hh * LANES
            o_ref[:, lo_c:lo_c + LANES] = rot.astype(o_ref.dtype)


def project(h, w_in, layer, col_blk, ncols, tn, tm, out_dtype, qk=None):
    nj = ncols // tn
    in_specs = [pl.BlockSpec((tm, D_MODEL), lambda j, t: (t, 0)),
                pl.BlockSpec((None, D_MODEL, tn), lambda j, t: (layer, 0, col_blk + j))]
    scratch = [pltpu.VMEM((D_MODEL, tn), BF16)]
    args = [h, w_in]
    if qk is None:
        body = _proj_plain_body
    else:
        gain, cos_t, sin_t = qk
        body = _proj_qk_body
        in_specs += [pl.BlockSpec((1, tn), lambda j, t: (0, j)),
                     pl.BlockSpec((tm, LANES), lambda j, t: (t, 0)),
                     pl.BlockSpec((tm, LANES), lambda j, t: (t, 0))]
        scratch.append(pltpu.VMEM((256, 256), BF16))
        args += [gain, cos_t, sin_t]
    return pl.pallas_call(
        body,
        grid=(nj, T // tm),
        in_specs=in_specs,
        out_specs=pl.BlockSpec((tm, tn), lambda j, t: (t, j)),
        out_shape=jax.ShapeDtypeStruct((T, ncols), out_dtype),
        scratch_shapes=scratch,
        compiler_params=_cp(2, 48),
        name="proj_qk" if qk is not None else "proj",
    )(*args)


def rope_tables():
    pos = jnp.concatenate([jnp.tile(jnp.arange(SEQ, dtype=I32), BATCH),
                           jnp.tile(PAST_LEN + jnp.arange(DEC_SEQ, dtype=I32), DEC_BATCH)])
    inv = ROPE_THETA ** (-jnp.arange(0, ROT_DIM, 2, dtype=F32) / ROT_DIM)
    ang = pos.astype(F32)[:, None] * inv[None, :]
    cos8, sin8 = jnp.cos(ang), jnp.sin(ang)
    rest = HEAD_DIM - ROT_DIM
    c64 = jnp.concatenate([cos8, cos8, jnp.ones((T, rest), F32)], axis=1)
    s64 = jnp.concatenate([-sin8, sin8, jnp.zeros((T, rest), F32)], axis=1)
    return jnp.tile(c64, (1, LANES // HEAD_DIM)), jnp.tile(s64, (1, LANES // HEAD_DIM))


def _lane_shift_variants(a):
    a0, a1 = a[:, :LANES], a[:, LANES:]
    r0 = pltpu.roll(a0, HEAD_DIM, axis=1)
    r1 = pltpu.roll(a1, HEAD_DIM, axis=1)
    low = lax.broadcasted_iota(I32, a0.shape, 1) < HEAD_DIM
    rot = jnp.concatenate([jnp.where(low, r1, r0), jnp.where(low, r0, r1)], axis=1)
    swap = lambda z: jnp.concatenate([z[:, LANES:], z[:, :LANES]], axis=1)
    return [a.astype(BF16), rot.astype(BF16), swap(a).astype(BF16), swap(rot).astype(BF16)]


def _attn_body(sinks_ref, q_ref, kh_ref, km_ref, vh_ref, vm_ref, o_ref, *, qrows, mask_first):
    i = pl.program_id(1)
    k_all = jnp.concatenate([kh_ref[...], km_ref[...]], axis=0)
    v_all = jnp.concatenate([vh_ref[...], vm_ref[...]], axis=0)
    n_groups = q_ref.shape[0] // qrows
    need = (n_groups - 1) * qrows + KEY_WIN
    if k_all.shape[0] < need:
        fill = jnp.zeros((need - k_all.shape[0], KV_W), F32)
        k_all = jnp.concatenate([k_all, fill], axis=0)
        v_all = jnp.concatenate([v_all, fill], axis=0)
    kvar = _lane_shift_variants(k_all)
    vvar = _lane_shift_variants(v_all)
    blk = lax.shift_right_logical(lax.broadcasted_iota(I32, (KEY_WIN, KV_W), 1), 6)
    zero = jnp.zeros((KEY_WIN, KV_W), BF16)
    q_chunk = lax.shift_right_logical(lax.broadcasted_iota(I32, (qrows, KEY_WIN), 0), 6)
    key = lax.broadcasted_iota(I32, (qrows, KEY_WIN), 1)
    band = (key >= q_chunk * CHUNK) & (key < q_chunk * CHUNK + WINDOW + CHUNK)
    for g in range(n_groups):
        w0 = g * qrows
        valid = band
        if mask_first and w0 < WINDOW:
            valid = band & (key >= jnp.where(i == 0, WINDOW - w0, 0))
        for x in range(N_KV_HEADS):
            kcat = jnp.concatenate([jnp.where(blk == j, kvar[(j - x) % GROUP][w0:w0 + KEY_WIN], zero)
                                    for j in range(GROUP)], axis=0)
            vcat = jnp.concatenate([jnp.where(blk == j, vvar[(j - x) % GROUP][w0:w0 + KEY_WIN], zero)
                                    for j in range(GROUP)], axis=0)
            qg = q_ref[w0:w0 + qrows, x * 256:(x + 1) * 256]
            s = lax.dot_general(qg, kcat, (((1,), (1,)), ((), ())),
                                preferred_element_type=F32) * (ATTN_SCALE * LOG2E)
            probs = []
            for j in range(GROUP):
                sj = jnp.where(valid, s[:, j * KEY_WIN:(j + 1) * KEY_WIN], NEG_INF)
                sink = sinks_ref[x * GROUP + j] * LOG2E
                m = jnp.maximum(jnp.max(sj, axis=1, keepdims=True), sink)
                p = jnp.exp2(sj - m)
                den = jnp.sum(p, axis=1, keepdims=True) + jnp.exp2(sink - m)
                probs.append((p / den).astype(BF16))
            og = _dot(jnp.concatenate(probs, axis=1), vcat)
            o_ref[w0:w0 + qrows, x * 256:(x + 1) * 256] = og.astype(o_ref.dtype)


def attention_prompt(sinks, q, k, v):
    rows = 4 * CHUNK
    steps = SEQ // rows
    hb = rows // WINDOW
    main = lambda b, i, s: (b * steps + i, 0)
    halo = lambda b, i, s: (jnp.maximum(b * steps * hb + i * hb - 1, b * steps * hb), 0)
    return pl.pallas_call(
        functools.partial(_attn_body, qrows=2 * CHUNK, mask_first=True),
        grid_spec=pltpu.PrefetchScalarGridSpec(
            num_scalar_prefetch=1,
            grid=(BATCH, steps),
            in_specs=[pl.BlockSpec((rows, ATTN_W), main),
                      pl.BlockSpec((WINDOW, KV_W), halo),
                      pl.BlockSpec((rows, KV_W), main),
                      pl.BlockSpec((WINDOW, KV_W), halo),
                      pl.BlockSpec((rows, KV_W), main)],
            out_specs=pl.BlockSpec((rows, ATTN_W), main)),
        out_shape=jax.ShapeDtypeStruct((TP, ATTN_W), BF16),
        compiler_params=_cp(2, 32),
        name="attn_prompt",
    )(sinks, q, k, k, v, v)


def attention_sample(sinks, q, k, v, cache_k, cache_v):
    first = TP // DEC_SEQ
    main = lambda b, i, s: (first + b, 0)
    halo = lambda b, i, s: (b, 0)
    return pl.pallas_call(
        functools.partial(_attn_body, qrows=CHUNK, mask_first=False),
        grid_spec=pltpu.PrefetchScalarGridSpec(
            num_scalar_prefetch=1,
            grid=(DEC_BATCH, 1),
            in_specs=[pl.BlockSpec((DEC_SEQ, ATTN_W), main),
                      pl.BlockSpec((WINDOW, KV_W), halo),
                      pl.BlockSpec((DEC_SEQ, KV_W), main),
                      pl.BlockSpec((WINDOW, KV_W), halo),
                      pl.BlockSpec((DEC_SEQ, KV_W), main)],
            out_specs=pl.BlockSpec((DEC_SEQ, ATTN_W), halo)),
        out_shape=jax.ShapeDtypeStruct((TS, ATTN_W), BF16),
        compiler_params=_cp(2, 32),
        name="attn_sample",
    )(sinks, q, cache_k, k, cache_v, v)


def _pool_body(u_ref, h_ref, pw_ref, sc_ref, y_ref, *, zero_first, full_count):
    i = pl.program_id(1)
    u = u_ref[...]
    tl = u.shape[0]
    halo = h_ref[...]
    if zero_first:
        halo = jnp.where(i == 0, 0.0, halo)
    ext = jnp.concatenate([halo, u], axis=0)
    pos = i * tl + lax.broadcasted_iota(I32, (tl, 1), 0)
    for g, w in enumerate(POOL_WINDOWS):
        sl = slice(g * POOL_GROUP_W, (g + 1) * POOL_GROUP_W)
        s = ext[:, sl]
        sh = 1
        while sh < w:
            s = s + pltpu.roll(s, sh, axis=0)
            sh *= 2
        tot = s[HALO:]
        cnt = float(w) if full_count else jnp.minimum(w, pos + 1).astype(F32)
        d = (tot / cnt - u[:, sl]).astype(BF16)
        yg = _dot(d, pw_ref[g].astype(BF16)) * sc_ref[:, sl]
        y_ref[:, sl] = yg.astype(y_ref.dtype)


def pool_prompt(u, pool_w, pool_scale, layer):
    tl = 512
    steps = SEQ // tl
    hpb = tl // HALO
    main = lambda b, i: (b * steps + i, 0)
    halo = lambda b, i: (jnp.maximum((b * steps + i) * hpb - 1, b * steps * hpb), 0)
    return pl.pallas_call(
        functools.partial(_pool_body, zero_first=True, full_count=False),
        grid=(BATCH, steps),
        in_specs=[pl.BlockSpec((tl, POOL_W), main),
                  pl.BlockSpec((HALO, POOL_W), halo),
                  pl.BlockSpec((None, 4, POOL_GROUP_W, POOL_GROUP_W), lambda b, i: (layer, 0, 0, 0)),
                  pl.BlockSpec((1, POOL_W), lambda b, i: (0, 0))],
        out_specs=pl.BlockSpec((tl, POOL_W), main),
        out_shape=jax.ShapeDtypeStruct((TP, POOL_W), BF16),
        compiler_params=_cp(2, 32),
        name="pool_prompt",
    )(u, u, pool_w, pool_scale)


def pool_sample(u, hist, pool_w, pool_scale, layer):
    first = TP // DEC_SEQ
    return pl.pallas_call(
        functools.partial(_pool_body, zero_first=False, full_count=True),
        grid=(DEC_BATCH, 1),
        in_specs=[pl.BlockSpec((DEC_SEQ, POOL_W), lambda b, i: (first + b, 0)),
                  pl.BlockSpec((HALO, POOL_W), lambda b, i: (b, 0)),
                  pl.BlockSpec((None, 4, POOL_GROUP_W, POOL_GROUP_W), lambda b, i: (layer, 0, 0, 0)),
                  pl.BlockSpec((1, POOL_W), lambda b, i: (0, 0))],
        out_specs=pl.BlockSpec((DEC_SEQ, POOL_W), lambda b, i: (b, 0)),
        out_shape=jax.ShapeDtypeStruct((TS, POOL_W), BF16),
        compiler_params=_cp(2, 32),
        name="pool_sample",
    )(u, hist, pool_w, pool_scale)


def _branch_body(h_ref, op_ref, os_ref, yp_ref, ys_ref, wga_ref, wgb_ref, wa_ref, wp_ref, out_ref,
                 wga, wgb, wa, wp):
    t = pl.program_id(1)

    @pl.when(t == 0)
    def _():
        wga[...] = wga_ref[...].astype(BF16)
        wgb[...] = wgb_ref[...].astype(BF16)
        wa[...] = wa_ref[...].astype(BF16)
        wp[...] = wp_ref[...].astype(BF16)

    h = h_ref[...]
    ga = jax.nn.sigmoid(_dot(h, wga[...]))
    gb = jax.nn.sigmoid(_dot(h, wgb[...]))
    a = _dot(_load_rows((op_ref, os_ref), t), wa[...])
    c = _dot(_load_rows((yp_ref, ys_ref), t), wp[...])
    out_ref[...] = (ga * a + gb * c).astype(out_ref.dtype)


def branch_mix(h, o_p, o_s, y_p, y_s, w_in, w_attn_br, w_pool_br, layer):
    tn, tm = 512, 512
    ga_blk = (ATTN_W + 2 * KV_W + POOL_W) // tn
    gb_blk = ga_blk + D_MODEL // tn
    return pl.pallas_call(
        _branch_body,
        grid=(D_MODEL // tn, T // tm),
        in_specs=([pl.BlockSpec((tm, D_MODEL), lambda j, t: (t, 0))]
                  + _row_specs(2, tm, ATTN_W, 2) + _row_specs(2, tm, POOL_W, 2)
                  + [pl.BlockSpec((None, D_MODEL, tn), lambda j, t: (layer, 0, ga_blk + j)),
                     pl.BlockSpec((None, D_MODEL, tn), lambda j, t: (layer, 0, gb_blk + j)),
                     pl.BlockSpec((None, ATTN_W, tn), lambda j, t: (layer, 0, j)),
                     pl.BlockSpec((None, POOL_W, tn), lambda j, t: (layer, 0, j))]),
        out_specs=pl.BlockSpec((tm, tn), lambda j, t: (t, j)),
        out_shape=jax.ShapeDtypeStruct((T, D_MODEL), BF16),
        scratch_shapes=[pltpu.VMEM((D_MODEL, tn), BF16), pltpu.VMEM((D_MODEL, tn), BF16),
                        pltpu.VMEM((ATTN_W, tn), BF16), pltpu.VMEM((POOL_W, tn), BF16)],
        compiler_params=_cp(2, 56),
        name="branch_mix",
    )(h, o_p, o_s, y_p, y_s, w_in, w_in, w_attn_br, w_pool_br)


def _wout_body(*refs, n_x, router):
    m_ref, *x_refs = refs[:1 + n_x]
    rest = refs[1 + n_x:]
    if router:
        w_ref, g_ref, r_ref, x1_ref, h2_ref, lg_ref, wbf = rest
    else:
        w_ref, g_ref, x1_ref, h2_ref, wbf = rest
    t = pl.program_id(0)

    @pl.when(t == 0)
    def _():
        wbf[...] = w_ref[...].astype(BF16)

    x1 = _load_rows(x_refs, t) + _dot(m_ref[...], wbf[...])
    x1_ref[...] = x1
    h2 = _rms(x1, g_ref[...])
    h2_ref[...] = h2.astype(h2_ref.dtype)
    if router:
        r = r_ref[...]
        r_hi = r.astype(BF16)
        r_lo = (r - r_hi.astype(F32)).astype(BF16)
        h_hi = h2.astype(BF16)
        h_lo = (h2 - h_hi.astype(F32)).astype(BF16)
        lg_ref[...] = _dot(h_hi, r_hi) + (_dot(h_lo, r_hi) + _dot(h_hi, r_lo))


def out_proj(mixed, x_parts, w_out, g, layer, router_w=None):
    tm = 256
    router = router_w is not None
    row = lambda t: (t, 0)
    const2 = lambda t: (0, 0)
    in_specs = ([pl.BlockSpec((tm, D_MODEL), row)] + _row_specs(len(x_parts), tm, D_MODEL, 1)
                + [pl.BlockSpec((None, D_MODEL, D_MODEL), lambda t: (layer, 0, 0),
                                pipeline_mode=pl.Buffered(1)),
                   pl.BlockSpec((1, D_MODEL), const2)])
    out_specs = [pl.BlockSpec((tm, D_MODEL), row), pl.BlockSpec((tm, D_MODEL), row)]
    out_shape = [jax.ShapeDtypeStruct((T, D_MODEL), F32),
                 jax.ShapeDtypeStruct((T, D_MODEL), F32 if router else BF16)]
    args = [mixed, *x_parts, w_out, g.reshape(1, D_MODEL)]
    if router:
        in_specs.append(pl.BlockSpec((D_MODEL, LANES), const2))
        out_specs.append(pl.BlockSpec((tm, LANES), row))
        out_shape.append(jax.ShapeDtypeStruct((T, LANES), F32))
        args.append(router_w)
    return pl.pallas_call(
        functools.partial(_wout_body, n_x=len(x_parts), router=router),
        grid=(T // tm,),
        in_specs=in_specs,
        out_specs=out_specs,
        out_shape=out_shape,
        scratch_shapes=[pltpu.VMEM((D_MODEL, D_MODEL), BF16)],
        compiler_params=_cp(1, 56),
        name="out_proj",
    )(*args)


def _ffn_up_body(h_ref, wg_ref, wu_ref, o_ref, wg, wu):
    @pl.when(pl.program_id(1) == 0)
    def _():
        wg[...] = wg_ref[...].astype(BF16)
        wu[...] = wu_ref[...].astype(BF16)

    h = h_ref[...]
    g = _dot(h, wg[...])
    o_ref[...] = (jax.nn.silu(g) * _dot(h, wu[...])).astype(o_ref.dtype)


def ffn_up(h2, w_gate_up, idx):
    tm = 1024
    return pl.pallas_call(
        _ffn_up_body,
        grid=(N_FF_BLK, T // tm),
        in_specs=[pl.BlockSpec((tm, D_MODEL), lambda j, t: (t, 0)),
                  pl.BlockSpec((None, D_MODEL, FF_BLK), lambda j, t: (idx, 0, j)),
                  pl.BlockSpec((None, D_MODEL, FF_BLK), lambda j, t: (idx, 0, N_FF_BLK + j))],
        out_specs=pl.BlockSpec((tm, FF_BLK), lambda j, t: (t, j)),
        out_shape=jax.ShapeDtypeStruct((T, D_FF), BF16),
        scratch_shapes=[pltpu.VMEM((D_MODEL, FF_BLK), BF16), pltpu.VMEM((D_MODEL, FF_BLK), BF16)],
        compiler_params=_cp(2, 48),
        name="ffn_up",
    )(h2, w_gate_up, w_gate_up)


def _ffn_down_body(a_ref, w_ref, x_ref, o_ref, wbf):
    @pl.when(pl.program_id(1) == 0)
    def _():
        wbf[...] = w_ref[...].astype(BF16)

    o_ref[...] = x_ref[...] + _dot(a_ref[...], wbf[...])


def ffn_down(act, w_down, idx, x1):
    tm, tn = 512, 512
    return pl.pallas_call(
        _ffn_down_body,
        grid=(D_MODEL // tn, T // tm),
        in_specs=[pl.BlockSpec((tm, D_FF), lambda j, t: (t, 0)),
                  pl.BlockSpec((None, D_FF, tn), lambda j, t: (idx, 0, j)),
                  pl.BlockSpec((tm, tn), lambda j, t: (t, j))],
        out_specs=pl.BlockSpec((tm, tn), lambda j, t: (t, j)),
        out_shape=jax.ShapeDtypeStruct((T, D_MODEL), F32),
        scratch_shapes=[pltpu.VMEM((D_FF, tn), BF16)],
        compiler_params=_cp(2, 56),
        name="ffn_down",
    )(act, w_down, x1)


def _ple_body(h_ref, pp_ref, ps_ref, x_ref, wg_ref, wp_ref, *rest, last):
    t = pl.program_id(0)
    if last:
        yp_ref, ys_ref, wg, wp = rest
    else:
        gn_ref, x3_ref, hn_ref, wg, wp = rest

    @pl.when(t == 0)
    def _():
        wg[...] = wg_ref[...].astype(BF16)
        wp[...] = wp_ref[...].astype(BF16)

    gate = jax.nn.sigmoid(_dot(h_ref[...], wg[...]))
    p = _load_rows((pp_ref, ps_ref), t).astype(BF16)
    x3 = x_ref[...] + _dot(p, wp[...]) * gate
    if last:
        pb = TP // x3.shape[0]

        @pl.when(t < pb)
        def _():
            yp_ref[...] = x3

        @pl.when(t >= pb)
        def _():
            ys_ref[...] = x3
    else:
        x3_ref[...] = x3
        hn_ref[...] = _rms(x3, gn_ref[...]).astype(hn_ref.dtype)


def ple(h3, p_prompt, p_sample, x2, w_ple_gate, w_ple, layer, g_next):
    tm = 256
    last = g_next is None
    pb = TP // tm
    row = lambda t: (t, 0)
    prow = lambda t: (jnp.minimum(t, pb - 1), 0)
    srow = lambda t: (jnp.maximum(t - pb, 0), 0)
    in_specs = [pl.BlockSpec((tm, D_MODEL), row),
                pl.BlockSpec((None, tm, PLE_DIM), lambda t: (layer, jnp.minimum(t, pb - 1), 0)),
                pl.BlockSpec((None, tm, PLE_DIM), lambda t: (layer, jnp.maximum(t - pb, 0), 0)),
                pl.BlockSpec((tm, D_MODEL), row),
                pl.BlockSpec((None, D_MODEL, D_MODEL), lambda t: (layer, 0, 0), pipeline_mode=pl.Buffered(1)),
                pl.BlockSpec((None, PLE_DIM, D_MODEL), lambda t: (layer, 0, 0), pipeline_mode=pl.Buffered(1))]
    args = [h3, p_prompt.reshape(DEPTH, TP, PLE_DIM), p_sample.reshape(DEPTH, TS, PLE_DIM), x2, w_ple_gate, w_ple]
    if last:
        out_specs = [pl.BlockSpec((tm, D_MODEL), prow), pl.BlockSpec((tm, D_MODEL), srow)]
        out_shape = [jax.ShapeDtypeStruct((TP, D_MODEL), F32), jax.ShapeDtypeStruct((TS, D_MODEL), F32)]
    else:
        in_specs.append(pl.BlockSpec((1, D_MODEL), lambda t: (0, 0)))
        args.append(g_next.reshape(1, D_MODEL))
        out_specs = [pl.BlockSpec((tm, D_MODEL), row), pl.BlockSpec((tm, D_MODEL), row)]
        out_shape = [jax.ShapeDtypeStruct((T, D_MODEL), F32), jax.ShapeDtypeStruct((T, D_MODEL), BF16)]
    return pl.pallas_call(
        functools.partial(_ple_body, last=last),
        grid=(T // tm,),
        in_specs=in_specs,
        out_specs=out_specs,
        out_shape=out_shape,
        scratch_shapes=[pltpu.VMEM((D_MODEL, D_MODEL), BF16), pltpu.VMEM((PLE_DIM, D_MODEL), BF16)],
        compiler_params=_cp(1, 56),
        name="ple",
    )(*args)


def _router_body(lg_ref, meta_ref, cnt_ref, carry):
    @pl.when(pl.program_id(0) == 0)
    def _():
        carry[...] = jnp.zeros_like(carry)

    lg = lg_ref[...]
    tm = lg.shape[0]
    lane = lax.broadcasted_iota(I32, (tm, LANES), 1)
    lane_f = lane.astype(F32)
    big = float(LANES)
    l1 = jnp.where(lane < N_EXPERTS, lg, -jnp.inf)
    m1 = jnp.max(l1, axis=1, keepdims=True)
    i1 = jnp.min(jnp.where(l1 == m1, lane_f, big), axis=1, keepdims=True)
    oh1 = lane_f == i1
    l2 = jnp.where(oh1, -jnp.inf, l1)
    m2 = jnp.max(l2, axis=1, keepdims=True)
    i2 = jnp.min(jnp.where(l2 == m2, lane_f, big), axis=1, keepdims=True)
    oh2 = lane_f == i2
    e = jnp.exp(m2 - m1)
    w1 = 1.0 / (1.0 + e)
    w2 = e / (1.0 + e)
    oh = jnp.where(oh1 | oh2, 1.0, 0.0)
    r = lax.broadcasted_iota(I32, (tm, tm), 0)
    c = lax.broadcasted_iota(I32, (tm, tm), 1)
    tri = jnp.where(c < r, 1.0, 0.0).astype(BF16)
    before = _dot(tri, oh.astype(BF16)) + carry[0:1, :]
    r1 = jnp.sum(jnp.where(oh1, before, 0.0), axis=1, keepdims=True)
    r2 = jnp.sum(jnp.where(oh2, before, 0.0), axis=1, keepdims=True)
    meta = jnp.where(lane == 0, i1, jnp.where(lane == 1, i2, jnp.where(lane == 2, w1,
           jnp.where(lane == 3, w2, jnp.where(lane == 4, r1, jnp.where(lane == 5, r2, 0.0))))))
    meta_ref[...] = meta
    total = carry[0:1, :] + jnp.sum(oh, axis=0, keepdims=True)
    carry[...] = jnp.broadcast_to(total, carry.shape)
    cnt_ref[...] = jnp.broadcast_to(total, cnt_ref.shape)


def route(logits):
    tm = 512
    return pl.pallas_call(
        _router_body,
        grid=(T // tm,),
        in_specs=[pl.BlockSpec((tm, LANES), lambda t: (t, 0))],
        out_specs=[pl.BlockSpec((tm, LANES), lambda t: (t, 0)),
                   pl.BlockSpec((8, LANES), lambda t: (0, 0))],
        out_shape=[jax.ShapeDtypeStruct((T, LANES), F32), jax.ShapeDtypeStruct((8, LANES), F32)],
        scratch_shapes=[pltpu.VMEM((8, LANES), F32)],
        compiler_params=_cp(1, 32),
        name="moe_route",
    )(logits)


def moe_plan(meta, cnt_rows):
    i1 = meta[:, 0].astype(I32)
    i2 = meta[:, 1].astype(I32)
    r1 = meta[:, 4].astype(I32)
    r2 = meta[:, 5].astype(I32)
    cnt = cnt_rows[0, :N_EXPERTS].astype(I32)
    tiles_e = (cnt + MOE_TILE - 1) // MOE_TILE
    cum_tiles = jnp.cumsum(tiles_e)
    pad_off = (cum_tiles - tiles_e) * MOE_TILE
    n_tiles = cum_tiles[-1:]
    e_ids = jnp.arange(N_EXPERTS, dtype=I32)
    pick = lambda idx: jnp.sum(jnp.where(idx[:, None] == e_ids[None, :], pad_off[None, :], 0), axis=1)
    pos1 = pick(i1) + r1
    pos2 = pick(i2) + r2
    last_e = jnp.max(jnp.where(cnt > 0, e_ids, 0))
    tile_ids = jnp.arange(MOE_TILES, dtype=I32)
    tile_e = jnp.minimum(jnp.sum(tile_ids[:, None] >= cum_tiles[None, :], axis=1).astype(I32), last_e)
    pad_rows = (pad_off + cnt)[:, None] + jnp.arange(MOE_TILE - 1, dtype=I32)[None, :]
    pad_keys = jnp.where(pad_rows < (pad_off + tiles_e * MOE_TILE)[:, None], pad_rows, 2 * MOE_ROWS).reshape(-1)
    tok = jnp.arange(T, dtype=I32)
    keys = jnp.concatenate([pos1, pos2, pad_keys])
    vals = jnp.concatenate([tok, tok, jnp.zeros((MOE_PAD,), I32)])
    row_tok = lax.sort((keys, vals), num_keys=1)[1][:MOE_ROWS]
    return pos1, pos2, tile_e, n_tiles, row_tok


def _row_copy(src_hbm, dst_vmem, sem, src_row, dst_row):
    return pltpu.make_async_copy(src_hbm.at[pl.ds(src_row, 1)], dst_vmem.at[pl.ds(dst_row, 1)], sem)


def _start_rows(idx_ref, src_hbm, dst_vmem, sem):
    def body(r, carry):
        _row_copy(src_hbm, dst_vmem, sem, idx_ref[0, 0, r], r).start()
        return carry

    lax.fori_loop(0, dst_vmem.shape[0], body, 0, unroll=8)


def _wait_rows(src_hbm, dst_vmem, sem):
    def body(r, carry):
        _row_copy(src_hbm, dst_vmem, sem, 0, r).wait()
        return carry

    lax.fori_loop(0, dst_vmem.shape[0], body, 0, unroll=8)


def _zero_dead_tile(nt_ref, r, o_ref):
    @pl.when(r >= nt_ref[0])
    def _():
        o_ref[...] = jnp.zeros_like(o_ref)


def _dispatch_body(nt_ref, tok_ref, nxt_ref, h_hbm, o_ref, buf, sem):
    r = pl.program_id(0)
    slot = r % 2

    @pl.when(r == 0)
    def _():
        _start_rows(tok_ref, h_hbm, buf.at[0], sem.at[0])

    @pl.when(r + 1 < nt_ref[0])
    def _():
        _start_rows(nxt_ref, h_hbm, buf.at[1 - slot], sem.at[1 - slot])

    @pl.when(r < nt_ref[0])
    def _():
        _wait_rows(h_hbm, buf.at[slot], sem.at[slot])
        o_ref[...] = buf[slot].astype(o_ref.dtype)

    _zero_dead_tile(nt_ref, r, o_ref)


def moe_dispatch(n_tiles, row_tok, h2f):
    steps = MOE_ROWS // GATHER_ROWS
    smem = lambda off: pl.BlockSpec((1, 1, GATHER_ROWS),
                                    lambda r, nt: (jnp.minimum(r + off, steps - 1), 0, 0),
                                    memory_space=pltpu.SMEM)
    tok3 = row_tok.reshape(steps, 1, GATHER_ROWS)
    return pl.pallas_call(
        _dispatch_body,
        grid_spec=pltpu.PrefetchScalarGridSpec(
            num_scalar_prefetch=1,
            grid=(steps,),
            in_specs=[smem(0), smem(1), pl.BlockSpec(memory_space=pl.ANY)],
            out_specs=pl.BlockSpec((GATHER_ROWS, D_MODEL), lambda r, nt: (r, 0)),
            scratch_shapes=[pltpu.VMEM((2, GATHER_ROWS, D_MODEL), F32), pltpu.SemaphoreType.DMA((2,))]),
        out_shape=jax.ShapeDtypeStruct((MOE_ROWS, D_MODEL), BF16),
        compiler_params=_cp(1, 32),
        name="moe_dispatch",
    )(n_tiles * (MOE_TILE // GATHER_ROWS), tok3, tok3, h2f)


def _expert_changed(te_ref, r):
    return jnp.logical_or(r == 0, te_ref[r] != te_ref[jnp.maximum(r - 1, 0)])


def _moe_up_body(te_ref, nt_ref, a_ref, wg_ref, wu_ref, o_ref, wg, wu):
    r = pl.program_id(1)

    @pl.when(r < nt_ref[0])
    def _():
        @pl.when(_expert_changed(te_ref, r))
        def _():
            wg[...] = wg_ref[...].astype(BF16)
            wu[...] = wu_ref[...].astype(BF16)

        a = a_ref[...]
        g = _dot(a, wg[...])
        o_ref[...] = (jax.nn.silu(g) * _dot(a, wu[...])).astype(o_ref.dtype)

    _zero_dead_tile(nt_ref, r, o_ref)


def moe_up(tile_e, n_tiles, xs, w_gate_up, idx):
    live = lambda j, r, te, nt: (jnp.minimum(r, nt[0] - 1), 0)
    return pl.pallas_call(
        _moe_up_body,
        grid_spec=pltpu.PrefetchScalarGridSpec(
            num_scalar_prefetch=2,
            grid=(N_FF_BLK, MOE_TILES),
            in_specs=[pl.BlockSpec((MOE_TILE, D_MODEL), live),
                      pl.BlockSpec((None, None, D_MODEL, FF_BLK),
                                   lambda j, r, te, nt: (idx, te[r], 0, j)),
                      pl.BlockSpec((None, None, D_MODEL, FF_BLK),
                                   lambda j, r, te, nt: (idx, te[r], 0, N_FF_BLK + j))],
            out_specs=pl.BlockSpec((MOE_TILE, FF_BLK), lambda j, r, te, nt: (r, j)),
            scratch_shapes=[pltpu.VMEM((D_MODEL, FF_BLK), BF16), pltpu.VMEM((D_MODEL, FF_BLK), BF16)]),
        out_shape=jax.ShapeDtypeStruct((MOE_ROWS, D_FF), BF16),
        compiler_params=_cp(2, 48),
        name="moe_up",
    )(tile_e, n_tiles, xs, w_gate_up, w_gate_up)


def _moe_down_body(te_ref, nt_ref, a_ref, w_ref, o_ref, wbf):
    r = pl.program_id(1)

    @pl.when(r < nt_ref[0])
    def _():
        @pl.when(_expert_changed(te_ref, r))
        def _():
            wbf[...] = w_ref[...].astype(BF16)

        o_ref[...] = _dot(a_ref[...], wbf[...])

    _zero_dead_tile(nt_ref, r, o_ref)


def moe_down(tile_e, n_tiles, act, w_down, idx):
    tn = 512
    return pl.pallas_call(
        _moe_down_body,
        grid_spec=pltpu.PrefetchScalarGridSpec(
            num_scalar_prefetch=2,
            grid=(D_MODEL // tn, MOE_TILES),
            in_specs=[pl.BlockSpec((MOE_TILE, D_FF), lambda j, r, te, nt: (jnp.minimum(r, nt[0] - 1), 0)),
                      pl.BlockSpec((None, None, D_FF, tn), lambda j, r, te, nt: (idx, te[r], 0, j))],
            out_specs=pl.BlockSpec((MOE_TILE, tn), lambda j, r, te, nt: (r, j)),
            scratch_shapes=[pltpu.VMEM((D_FF, tn), BF16)]),
        out_shape=jax.ShapeDtypeStruct((MOE_ROWS, D_MODEL), F32),
        compiler_params=_cp(2, 56),
        name="moe_down",
    )(tile_e, n_tiles, act, w_down)


def _combine_body(p1_ref, p2_ref, n1_ref, n2_ref, ys_hbm, meta_ref, x_ref, g_ref, x2_ref, h3_ref,
                  b1, b2, sem1, sem2):
    t = pl.program_id(0)
    slot = t % 2

    @pl.when(t == 0)
    def _():
        _start_rows(p1_ref, ys_hbm, b1.at[0], sem1.at[0])
        _start_rows(p2_ref, ys_hbm, b2.at[0], sem2.at[0])

    @pl.when(t + 1 < pl.num_programs(0))
    def _():
        _start_rows(n1_ref, ys_hbm, b1.at[1 - slot], sem1.at[1 - slot])
        _start_rows(n2_ref, ys_hbm, b2.at[1 - slot], sem2.at[1 - slot])

    _wait_rows(ys_hbm, b1.at[slot], sem1.at[slot])
    _wait_rows(ys_hbm, b2.at[slot], sem2.at[slot])
    meta = meta_ref[...]
    w1 = meta[:, 2:3]
    w2 = meta[:, 3:4]
    x2 = x_ref[...] + (w1 * b1[slot] + w2 * b2[slot])
    x2_ref[...] = x2
    h3_ref[...] = _rms(x2, g_ref[...]).astype(h3_ref.dtype)


def moe_combine(pos1, pos2, ys, meta, x1, g):
    tm = GATHER_ROWS
    nblk = T // tm
    row = lambda t: (t, 0)
    smem = lambda off: pl.BlockSpec((1, 1, tm), lambda t: (jnp.minimum(t + off, nblk - 1), 0, 0),
                                    memory_space=pltpu.SMEM)
    p1 = pos1.reshape(nblk, 1, tm)
    p2 = pos2.reshape(nblk, 1, tm)
    return pl.pallas_call(
        _combine_body,
        grid=(nblk,),
        in_specs=[smem(0), smem(0), smem(1), smem(1),
                  pl.BlockSpec(memory_space=pl.ANY),
                  pl.BlockSpec((tm, LANES), row),
                  pl.BlockSpec((tm, D_MODEL), row),
                  pl.BlockSpec((1, D_MODEL), lambda t: (0, 0))],
        out_specs=[pl.BlockSpec((tm, D_MODEL), row), pl.BlockSpec((tm, D_MODEL), row)],
        out_shape=[jax.ShapeDtypeStruct((T, D_MODEL), F32), jax.ShapeDtypeStruct((T, D_MODEL), BF16)],
        scratch_shapes=[pltpu.VMEM((2, tm, D_MODEL), F32), pltpu.VMEM((2, tm, D_MODEL), F32),
                        pltpu.SemaphoreType.DMA((2,)), pltpu.SemaphoreType.DMA((2,))],
        compiler_params=_cp(1, 48),
        name="moe_combine",
    )(p1, p2, p1, p2, ys, meta, x1, g.reshape(1, D_MODEL))


def _last_rows(a, batch, seq, n):
    return a.reshape(batch, seq, a.shape[-1])[:, seq - n:]


def kernel(x_prompt, x_sample, cache_k, cache_v, state_pool, p_prompt, p_sample, norm_mix, w_in, q_norm, k_norm, sinks, pool_w, pool_scale, w_attn_br, w_pool_br, w_out, norm_ffn, ffn_w_gate_up, ffn_w_down, moe_router, moe_w_gate_up, moe_w_down, norm_ple, w_ple, w_ple_gate):
    x_parts = (x_prompt.reshape(TP, D_MODEL), x_sample.reshape(TS, D_MODEL))
    cos_t, sin_t = rope_tables()
    outs = {k: [] for k in ("kp", "vp", "pp", "ks", "vs", "ps")}
    h = rmsnorm_bf16(x_parts, norm_mix[0])

    for l in range(DEPTH):
        q_gain = jnp.tile(q_norm[l], ATTN_W // HEAD_DIM).reshape(1, ATTN_W)
        k_gain = jnp.tile(k_norm[l], KV_W // HEAD_DIM).reshape(1, KV_W)
        q = project(h, w_in, l, 0, ATTN_W, 512, 512, BF16, qk=(q_gain, cos_t, sin_t))
        k = project(h, w_in, l, ATTN_W // KV_W, KV_W, KV_W, 512, F32, qk=(k_gain, cos_t, sin_t))
        v = project(h, w_in, l, (ATTN_W + KV_W) // KV_W, KV_W, KV_W, 1024, F32)
        u = project(h, w_in, l, (ATTN_W + 2 * KV_W) // 512, POOL_W, 512, 1024, F32)

        ck = cache_k[l].reshape(DEC_BATCH * WINDOW, KV_W)
        cv = cache_v[l].reshape(DEC_BATCH * WINDOW, KV_W)
        o_p = attention_prompt(sinks[l], q, k, v)
        o_s = attention_sample(sinks[l], q, k, v, ck, cv)

        hist = jnp.pad(state_pool[l], ((0, 0), (HALO - POOL_HIST, 0), (0, 0))).reshape(DEC_BATCH * HALO, POOL_W)
        scale = pool_scale[l].reshape(1, POOL_W)
        y_p = pool_prompt(u, pool_w, scale, l)
        y_s = pool_sample(u, hist, pool_w, scale, l)

        mixed = branch_mix(h, o_p, o_s, y_p, y_s, w_in, w_attn_br, w_pool_br, l)

        if l % 2 == 0:
            x1, h2 = out_proj(mixed, x_parts, w_out, norm_ffn[l], l)
            act = ffn_up(h2, ffn_w_gate_up, l // 2)
            x2 = ffn_down(act, ffn_w_down, l // 2, x1)
            h3 = rmsnorm_bf16((x2,), norm_ple[l])
        else:
            router_w = jnp.pad(moe_router[l // 2], ((0, 0), (0, LANES - N_EXPERTS)))
            x1, h2f, logits = out_proj(mixed, x_parts, w_out, norm_ffn[l], l, router_w=router_w)
            meta, cnt_rows = route(logits)
            pos1, pos2, tile_e, n_tiles, row_tok = moe_plan(meta, cnt_rows)
            xs = moe_dispatch(n_tiles, row_tok, h2f)
            act = moe_up(tile_e, n_tiles, xs, moe_w_gate_up, l // 2)
            ys = moe_down(tile_e, n_tiles, act, moe_w_down, l // 2)
            x2, h3 = moe_combine(pos1, pos2, ys, meta, x1, norm_ple[l])

        if l + 1 < DEPTH:
            x, h = ple(h3, p_prompt, p_sample, x2, w_ple_gate, w_ple, l, norm_mix[l + 1])
            x_parts = (x,)
        else:
            y_prompt, y_sample = ple(h3, p_prompt, p_sample, x2, w_ple_gate, w_ple, l, None)

        kv_shape = (N_KV_HEADS, HEAD_DIM)
        outs["kp"].append(_last_rows(k[:TP], BATCH, SEQ, WINDOW).reshape(BATCH, WINDOW, *kv_shape))
        outs["vp"].append(_last_rows(v[:TP], BATCH, SEQ, WINDOW).reshape(BATCH, WINDOW, *kv_shape))
        ksn = k[TP:].reshape(DEC_BATCH, DEC_SEQ, *kv_shape)
        vsn = v[TP:].reshape(DEC_BATCH, DEC_SEQ, *kv_shape)
        outs["ks"].append(jnp.concatenate([cache_k[l][:, DEC_SEQ:], ksn], axis=1))
        outs["vs"].append(jnp.concatenate([cache_v[l][:, DEC_SEQ:], vsn], axis=1))
        outs["pp"].append(_last_rows(u[:TP], BATCH, SEQ, POOL_HIST))
        us = u[TP:].reshape(DEC_BATCH, DEC_SEQ, POOL_W)
        outs["ps"].append(jnp.concatenate([state_pool[l], us], axis=1)[:, -POOL_HIST:])

    return (y_prompt.reshape(BATCH, SEQ, D_MODEL), y_sample.reshape(DEC_BATCH, DEC_SEQ, D_MODEL),
            jnp.stack(outs["kp"]), jnp.stack(outs["vp"]), jnp.stack(outs["pp"]),
            jnp.stack(outs["ks"]), jnp.stack(outs["vs"]), jnp.stack(outs["ps"]))
```

```python
import functools

import jax
import jax.numpy as jnp
from jax import lax
from jax.experimental import pallas as pl
from jax.experimental.pallas import tpu as pltpu

F32 = jnp.float32
BF16 = jnp.bfloat16
I32 = jnp.int32

D_MODEL = 2048
BATCH = 2
SEQ = 4096
DEPTH = 2
DEC_BATCH = 16
DEC_SEQ = 64
PAST_LEN = 2048
CHUNK = 64
N_HEADS = 16
N_KV_HEADS = 4
HEAD_DIM = 64
GROUP = N_HEADS // N_KV_HEADS
WINDOW = 128
ROT_DIM = HEAD_DIM // 4
ROPE_THETA = 500000.0
ATTN_SCALE = HEAD_DIM ** -0.5
POOL_WINDOWS = (2, 4, 8, 16)
POOL_W = 1024
POOL_GROUP_W = 256
POOL_HIST = 15
ATTN_W = N_HEADS * HEAD_DIM
KV_W = N_KV_HEADS * HEAD_DIM
IN_W = ATTN_W + 2 * KV_W + POOL_W + 2 * D_MODEL
D_FF = 5632
N_EXPERTS = 8
PLE_DIM = 256
RMS_EPS = 1e-6
NEG_INF = -1e30

TP = BATCH * SEQ
TS = DEC_BATCH * DEC_SEQ
T = TP + TS

LANES = 128
HALO = 16
KEY_WIN = 256
MOE_TILE = 512
MOE_PAD = N_EXPERTS * (MOE_TILE - 1)
MOE_TILES = (2 * T + MOE_PAD) // MOE_TILE
MOE_ROWS = MOE_TILES * MOE_TILE
GATHER_STRIDE = 181
FF_BLK = 512
N_FF_BLK = D_FF // FF_BLK


def _cp(n_axes, vmem_mb):
    return pltpu.CompilerParams(dimension_semantics=("arbitrary",) * n_axes,
                                vmem_limit_bytes=vmem_mb * 2 ** 20)


def _dot(a, b):
    return jnp.dot(a, b, preferred_element_type=F32)


def _rms(x, g):
    ms = jnp.mean(x * x, axis=-1, keepdims=True)
    return (x * lax.rsqrt(ms + RMS_EPS)) * g


def _row_specs(n_parts, tm, width, nd):
    if n_parts == 1:
        return [pl.BlockSpec((tm, width), lambda *g: (g[nd - 1], 0))]
    pb = TP // tm
    return [pl.BlockSpec((tm, width), lambda *g: (jnp.minimum(g[nd - 1], pb - 1), 0)),
            pl.BlockSpec((tm, width), lambda *g: (jnp.maximum(g[nd - 1] - pb, 0), 0))]


def _load_rows(refs, t):
    if len(refs) == 1:
        return refs[0][...]
    tm = refs[0].shape[0]
    return jnp.where(t < TP // tm, refs[0][...], refs[1][...])


def _rmsnorm_body(*refs):
    *x_refs, g_ref, o_ref = refs
    o_ref[...] = _rms(_load_rows(x_refs, pl.program_id(0)), g_ref[...]).astype(o_ref.dtype)


def rmsnorm_bf16(x_parts, g):
    tm = 512
    return pl.pallas_call(
        _rmsnorm_body,
        grid=(T // tm,),
        in_specs=_row_specs(len(x_parts), tm, D_MODEL, 1) + [pl.BlockSpec((1, D_MODEL), lambda i: (0, 0))],
        out_specs=pl.BlockSpec((tm, D_MODEL), lambda i: (i, 0)),
        out_shape=jax.ShapeDtypeStruct((T, D_MODEL), BF16),
        compiler_params=_cp(1, 32),
        name="rmsnorm",
    )(*x_parts, g.reshape(1, D_MODEL))


def _proj_plain_body(a_ref, w_ref, o_ref, wbf):
    @pl.when(pl.program_id(1) == 0)
    def _():
        wbf[...] = w_ref[...].astype(BF16)

    o_ref[...] = _dot(a_ref[...], wbf[...]).astype(o_ref.dtype)


def _proj_qk_body(a_ref, w_ref, gain_ref, cos_ref, sin_ref, o_ref, wbf, seg):
    @pl.when(pl.program_id(1) == 0)
    def _():
        wbf[...] = w_ref[...].astype(BF16)
        r = lax.shift_right_logical(lax.broadcasted_iota(I32, seg.shape, 0), 6)
        c = lax.shift_right_logical(lax.broadcasted_iota(I32, seg.shape, 1), 6)
        seg[...] = jnp.where(r == c, 1.0, 0.0).astype(BF16)

    acc = _dot(a_ref[...], wbf[...])
    tm, tn = acc.shape
    sq = acc * acc
    hi = sq.astype(BF16)
    lo = (sq - hi.astype(F32)).astype(BF16)
    gain = gain_ref[...]
    cos = cos_ref[...]
    sin = sin_ref[...]
    first = (lax.broadcasted_iota(I32, (tm, LANES), 1) & (HEAD_DIM - 1)) < ROT_DIM // 2
    for cc in range(tn // 256):
        sl = slice(cc * 256, (cc + 1) * 256)
        ssum = _dot(hi[:, sl], seg[...]) + _dot(lo[:, sl], seg[...])
        y = (acc[:, sl] * lax.rsqrt(ssum * (1.0 / HEAD_DIM) + RMS_EPS)) * gain[:, sl]
        for hh in range(256 // LANES):
            yc = y[:, hh * LANES:(hh + 1) * LANES]
            up = pltpu.roll(yc, LANES - ROT_DIM // 2, axis=1)
            dn = pltpu.roll(yc, ROT_DIM // 2, axis=1)
            rot = yc * cos + jnp.where(first, up, dn) * sin
            lo_c = cc * 256 + hh * LANES
            o_ref[:, lo_c:lo_c + LANES] = rot.astype(o_ref.dtype)


def project(h, w_in, layer, col_blk, ncols, tn, tm, out_dtype, qk=None):
    nj = ncols // tn
    in_specs = [pl.BlockSpec((tm, D_MODEL), lambda j, t: (t, 0)),
                pl.BlockSpec((None, D_MODEL, tn), lambda j, t: (layer, 0, col_blk + j))]
    scratch = [pltpu.VMEM((D_MODEL, tn), BF16)]
    args = [h, w_in]
    if qk is None:
        body = _proj_plain_body
    else:
        gain, cos_t, sin_t = qk
        body = _proj_qk_body
        in_specs += [pl.BlockSpec((1, tn), lambda j, t: (0, j)),
                     pl.BlockSpec((tm, LANES), lambda j, t: (t, 0)),
                     pl.BlockSpec((tm, LANES), lambda j, t: (t, 0))]
        scratch.append(pltpu.VMEM((256, 256), BF16))
        args += [gain, cos_t, sin_t]
    return pl.pallas_call(
        body,
        grid=(nj, T // tm),
        in_specs=in_specs,
        out_specs=pl.BlockSpec((tm, tn), lambda j, t: (t, j)),
        out_shape=jax.ShapeDtypeStruct((T, ncols), out_dtype),
        scratch_shapes=scratch,
        compiler_params=_cp(2, 48),
        name="proj_qk" if qk is not None else "proj",
    )(*args)


def rope_tables():
    pos = jnp.concatenate([jnp.tile(jnp.arange(SEQ, dtype=I32), BATCH),
                           jnp.tile(PAST_LEN + jnp.arange(DEC_SEQ, dtype=I32), DEC_BATCH)])
    inv = ROPE_THETA ** (-jnp.arange(0, ROT_DIM, 2, dtype=F32) / ROT_DIM)
    ang = pos.astype(F32)[:, None] * inv[None, :]
    cos8, sin8 = jnp.cos(ang), jnp.sin(ang)
    rest = HEAD_DIM - ROT_DIM
    c64 = jnp.concatenate([cos8, cos8, jnp.ones((T, rest), F32)], axis=1)
    s64 = jnp.concatenate([-sin8, sin8, jnp.zeros((T, rest), F32)], axis=1)
    return jnp.tile(c64, (1, LANES // HEAD_DIM)), jnp.tile(s64, (1, LANES // HEAD_DIM))


def _lane_shift_variants(a):
    a0, a1 = a[:, :LANES], a[:, LANES:]
    r0 = pltpu.roll(a0, HEAD_DIM, axis=1)
    r1 = pltpu.roll(a1, HEAD_DIM, axis=1)
    low = lax.broadcasted_iota(I32, a0.shape, 1) < HEAD_DIM
    rot = jnp.concatenate([jnp.where(low, r1, r0), jnp.where(low, r0, r1)], axis=1)
    swap = lambda z: jnp.concatenate([z[:, LANES:], z[:, :LANES]], axis=1)
    return [a.astype(BF16), rot.astype(BF16), swap(a).astype(BF16), swap(rot).astype(BF16)]


def _attn_body(sinks_ref, q_ref, kh_ref, km_ref, vh_ref, vm_ref, o_ref, *, qrows, mask_first):
    i = pl.program_id(1)
    k_all = jnp.concatenate([kh_ref[...], km_ref[...]], axis=0)
    v_all = jnp.concatenate([vh_ref[...], vm_ref[...]], axis=0)
    n_groups = q_ref.shape[0] // qrows
    need = (n_groups - 1) * qrows + KEY_WIN
    if k_all.shape[0] < need:
        fill = jnp.zeros((need - k_all.shape[0], KV_W), F32)
        k_all = jnp.concatenate([k_all, fill], axis=0)
        v_all = jnp.concatenate([v_all, fill], axis=0)
    kvar = _lane_shift_variants(k_all)
    vvar = _lane_shift_variants(v_all)
    blk = lax.shift_right_logical(lax.broadcasted_iota(I32, (KEY_WIN, KV_W), 1), 6)
    zero = jnp.zeros((KEY_WIN, KV_W), BF16)
    q_chunk = lax.shift_right_logical(lax.broadcasted_iota(I32, (qrows, KEY_WIN), 0), 6)
    key = lax.broadcasted_iota(I32, (qrows, KEY_WIN), 1)
    band = (key >= q_chunk * CHUNK) & (key < q_chunk * CHUNK + WINDOW + CHUNK)
    for g in range(n_groups):
        w0 = g * qrows
        valid = band
        if mask_first and w0 < WINDOW:
            valid = band & (key >= jnp.where(i == 0, WINDOW - w0, 0))
        for x in range(N_KV_HEADS):
            kcat = jnp.concatenate([jnp.where(blk == j, kvar[(j - x) % GROUP][w0:w0 + KEY_WIN], zero)
                                    for j in range(GROUP)], axis=0)
            vcat = jnp.concatenate([jnp.where(blk == j, vvar[(j - x) % GROUP][w0:w0 + KEY_WIN], zero)
                                    for j in range(GROUP)], axis=0)
            qg = q_ref[w0:w0 + qrows, x * 256:(x + 1) * 256]
            s = lax.dot_general(qg, kcat, (((1,), (1,)), ((), ())),
                                preferred_element_type=F32) * ATTN_SCALE
            probs = []
            for j in range(GROUP):
                sj = jnp.where(valid, s[:, j * KEY_WIN:(j + 1) * KEY_WIN], NEG_INF)
                sink = sinks_ref[x * GROUP + j]
                m = jnp.maximum(jnp.max(sj, axis=1, keepdims=True), sink)
                p = jnp.exp(sj - m)
                den = jnp.sum(p, axis=1, keepdims=True) + jnp.exp(sink - m)
                probs.append((p / den).astype(BF16))
            og = _dot(jnp.concatenate(probs, axis=1), vcat)
            o_ref[w0:w0 + qrows, x * 256:(x + 1) * 256] = og.astype(o_ref.dtype)


def attention_prompt(sinks, q, k, v):
    rows = 4 * CHUNK
    steps = SEQ // rows
    hb = rows // WINDOW
    main = lambda b, i, s: (b * steps + i, 0)
    halo = lambda b, i, s: (jnp.maximum(b * steps * hb + i * hb - 1, b * steps * hb), 0)
    return pl.pallas_call(
        functools.partial(_attn_body, qrows=2 * CHUNK, mask_first=True),
        grid_spec=pltpu.PrefetchScalarGridSpec(
            num_scalar_prefetch=1,
            grid=(BATCH, steps),
            in_specs=[pl.BlockSpec((rows, ATTN_W), main),
                      pl.BlockSpec((WINDOW, KV_W), halo),
                      pl.BlockSpec((rows, KV_W), main),
                      pl.BlockSpec((WINDOW, KV_W), halo),
                      pl.BlockSpec((rows, KV_W), main)],
            out_specs=pl.BlockSpec((rows, ATTN_W), main)),
        out_shape=jax.ShapeDtypeStruct((TP, ATTN_W), BF16),
        compiler_params=_cp(2, 32),
        name="attn_prompt",
    )(sinks, q, k, k, v, v)


def attention_sample(sinks, q, k, v, cache_k, cache_v):
    first = TP // DEC_SEQ
    main = lambda b, i, s: (first + b, 0)
    halo = lambda b, i, s: (b, 0)
    return pl.pallas_call(
        functools.partial(_attn_body, qrows=CHUNK, mask_first=False),
        grid_spec=pltpu.PrefetchScalarGridSpec(
            num_scalar_prefetch=1,
            grid=(DEC_BATCH, 1),
            in_specs=[pl.BlockSpec((DEC_SEQ, ATTN_W), main),
                      pl.BlockSpec((WINDOW, KV_W), halo),
                      pl.BlockSpec((DEC_SEQ, KV_W), main),
                      pl.BlockSpec((WINDOW, KV_W), halo),
                      pl.BlockSpec((DEC_SEQ, KV_W), main)],
            out_specs=pl.BlockSpec((DEC_SEQ, ATTN_W), halo)),
        out_shape=jax.ShapeDtypeStruct((TS, ATTN_W), BF16),
        compiler_params=_cp(2, 32),
        name="attn_sample",
    )(sinks, q, cache_k, k, cache_v, v)


def _pool_body(u_ref, h_ref, pw_ref, sc_ref, y_ref, *, zero_first, full_count):
    i = pl.program_id(1)
    u = u_ref[...]
    tl = u.shape[0]
    halo = h_ref[...]
    if zero_first:
        halo = jnp.where(i == 0, 0.0, halo)
    ext = jnp.concatenate([halo, u], axis=0)
    pos = i * tl + lax.broadcasted_iota(I32, (tl, 1), 0)
    for g, w in enumerate(POOL_WINDOWS):
        sl = slice(g * POOL_GROUP_W, (g + 1) * POOL_GROUP_W)
        s = ext[:, sl]
        sh = 1
        while sh < w:
            s = s + pltpu.roll(s, sh, axis=0)
            sh *= 2
        tot = s[HALO:]
        cnt = float(w) if full_count else jnp.minimum(w, pos + 1).astype(F32)
        d = (tot / cnt - u[:, sl]).astype(BF16)
        yg = _dot(d, pw_ref[g].astype(BF16)) * sc_ref[:, sl]
        y_ref[:, sl] = yg.astype(y_ref.dtype)


def pool_prompt(u, pool_w, pool_scale, layer):
    tl = 512
    steps = SEQ // tl
    hpb = tl // HALO
    main = lambda b, i: (b * steps + i, 0)
    halo = lambda b, i: (jnp.maximum((b * steps + i) * hpb - 1, b * steps * hpb), 0)
    return pl.pallas_call(
        functools.partial(_pool_body, zero_first=True, full_count=False),
        grid=(BATCH, steps),
        in_specs=[pl.BlockSpec((tl, POOL_W), main),
                  pl.BlockSpec((HALO, POOL_W), halo),
                  pl.BlockSpec((None, 4, POOL_GROUP_W, POOL_GROUP_W), lambda b, i: (layer, 0, 0, 0)),
                  pl.BlockSpec((1, POOL_W), lambda b, i: (0, 0))],
        out_specs=pl.BlockSpec((tl, POOL_W), main),
        out_shape=jax.ShapeDtypeStruct((TP, POOL_W), BF16),
        compiler_params=_cp(2, 32),
        name="pool_prompt",
    )(u, u, pool_w, pool_scale)


def pool_sample(u, hist, pool_w, pool_scale, layer):
    first = TP // DEC_SEQ
    return pl.pallas_call(
        functools.partial(_pool_body, zero_first=False, full_count=True),
        grid=(DEC_BATCH, 1),
        in_specs=[pl.BlockSpec((DEC_SEQ, POOL_W), lambda b, i: (first + b, 0)),
                  pl.BlockSpec((HALO, POOL_W), lambda b, i: (b, 0)),
                  pl.BlockSpec((None, 4, POOL_GROUP_W, POOL_GROUP_W), lambda b, i: (layer, 0, 0, 0)),
                  pl.BlockSpec((1, POOL_W), lambda b, i: (0, 0))],
        out_specs=pl.BlockSpec((DEC_SEQ, POOL_W), lambda b, i: (b, 0)),
        out_shape=jax.ShapeDtypeStruct((TS, POOL_W), BF16),
        compiler_params=_cp(2, 32),
        name="pool_sample",
    )(u, hist, pool_w, pool_scale)


def _branch_body(h_ref, op_ref, os_ref, yp_ref, ys_ref, wga_ref, wgb_ref, wa_ref, wp_ref, out_ref,
                 wga, wgb, wa, wp):
    t = pl.program_id(1)

    @pl.when(t == 0)
    def _():
        wga[...] = wga_ref[...].astype(BF16)
        wgb[...] = wgb_ref[...].astype(BF16)
        wa[...] = wa_ref[...].astype(BF16)
        wp[...] = wp_ref[...].astype(BF16)

    h = h_ref[...]
    ga = jax.nn.sigmoid(_dot(h, wga[...]))
    gb = jax.nn.sigmoid(_dot(h, wgb[...]))
    a = _dot(_load_rows((op_ref, os_ref), t), wa[...])
    c = _dot(_load_rows((yp_ref, ys_ref), t), wp[...])
    out_ref[...] = (ga * a + gb * c).astype(out_ref.dtype)


def branch_mix(h, o_p, o_s, y_p, y_s, w_in, w_attn_br, w_pool_br, layer):
    tn, tm = 512, 512
    ga_blk = (ATTN_W + 2 * KV_W + POOL_W) // tn
    gb_blk = ga_blk + D_MODEL // tn
    return pl.pallas_call(
        _branch_body,
        grid=(D_MODEL // tn, T // tm),
        in_specs=([pl.BlockSpec((tm, D_MODEL), lambda j, t: (t, 0))]
                  + _row_specs(2, tm, ATTN_W, 2) + _row_specs(2, tm, POOL_W, 2)
                  + [pl.BlockSpec((None, D_MODEL, tn), lambda j, t: (layer, 0, ga_blk + j)),
                     pl.BlockSpec((None, D_MODEL, tn), lambda j, t: (layer, 0, gb_blk + j)),
                     pl.BlockSpec((None, ATTN_W, tn), lambda j, t: (layer, 0, j)),
                     pl.BlockSpec((None, POOL_W, tn), lambda j, t: (layer, 0, j))]),
        out_specs=pl.BlockSpec((tm, tn), lambda j, t: (t, j)),
        out_shape=jax.ShapeDtypeStruct((T, D_MODEL), BF16),
        scratch_shapes=[pltpu.VMEM((D_MODEL, tn), BF16), pltpu.VMEM((D_MODEL, tn), BF16),
                        pltpu.VMEM((ATTN_W, tn), BF16), pltpu.VMEM((POOL_W, tn), BF16)],
        compiler_params=_cp(2, 56),
        name="branch_mix",
    )(h, o_p, o_s, y_p, y_s, w_in, w_in, w_attn_br, w_pool_br)


def _wout_body(*refs, n_x, router):
    m_ref, *x_refs = refs[:1 + n_x]
    rest = refs[1 + n_x:]
    if router:
        w_ref, g_ref, r_ref, x1_ref, h2_ref, lg_ref, wbf = rest
    else:
        w_ref, g_ref, x1_ref, h2_ref, wbf = rest
    t = pl.program_id(0)

    @pl.when(t == 0)
    def _():
        wbf[...] = w_ref[...].astype(BF16)

    x1 = _load_rows(x_refs, t) + _dot(m_ref[...], wbf[...])
    x1_ref[...] = x1
    h2 = _rms(x1, g_ref[...])
    h2_ref[...] = h2.astype(h2_ref.dtype)
    if router:
        r = r_ref[...]
        r_hi = r.astype(BF16)
        r_lo = (r - r_hi.astype(F32)).astype(BF16)
        h_hi = h2.astype(BF16)
        h_lo = (h2 - h_hi.astype(F32)).astype(BF16)
        lg_ref[...] = _dot(h_hi, r_hi) + (_dot(h_lo, r_hi) + _dot(h_hi, r_lo))


def out_proj(mixed, x_parts, w_out, g, layer, router_w=None):
    tm = 256
    router = router_w is not None
    row = lambda t: (t, 0)
    const2 = lambda t: (0, 0)
    in_specs = ([pl.BlockSpec((tm, D_MODEL), row)] + _row_specs(len(x_parts), tm, D_MODEL, 1)
                + [pl.BlockSpec((None, D_MODEL, D_MODEL), lambda t: (layer, 0, 0),
                                pipeline_mode=pl.Buffered(1)),
                   pl.BlockSpec((1, D_MODEL), const2)])
    out_specs = [pl.BlockSpec((tm, D_MODEL), row), pl.BlockSpec((tm, D_MODEL), row)]
    out_shape = [jax.ShapeDtypeStruct((T, D_MODEL), F32),
                 jax.ShapeDtypeStruct((T, D_MODEL), F32 if router else BF16)]
    args = [mixed, *x_parts, w_out, g.reshape(1, D_MODEL)]
    if router:
        in_specs.append(pl.BlockSpec((D_MODEL, LANES), const2))
        out_specs.append(pl.BlockSpec((tm, LANES), row))
        out_shape.append(jax.ShapeDtypeStruct((T, LANES), F32))
        args.append(router_w)
    return pl.pallas_call(
        functools.partial(_wout_body, n_x=len(x_parts), router=router),
        grid=(T // tm,),
        in_specs=in_specs,
        out_specs=out_specs,
        out_shape=out_shape,
        scratch_shapes=[pltpu.VMEM((D_MODEL, D_MODEL), BF16)],
        compiler_params=_cp(1, 56),
        name="out_proj",
    )(*args)


def _ffn_up_body(h_ref, wg_ref, wu_ref, o_ref, wg, wu):
    @pl.when(pl.program_id(1) == 0)
    def _():
        wg[...] = wg_ref[...].astype(BF16)
        wu[...] = wu_ref[...].astype(BF16)

    h = h_ref[...]
    g = _dot(h, wg[...])
    o_ref[...] = (jax.nn.silu(g) * _dot(h, wu[...])).astype(o_ref.dtype)


def ffn_up(h2, w_gate_up, idx):
    tm = 1024
    return pl.pallas_call(
        _ffn_up_body,
        grid=(N_FF_BLK, T // tm),
        in_specs=[pl.BlockSpec((tm, D_MODEL), lambda j, t: (t, 0)),
                  pl.BlockSpec((None, D_MODEL, FF_BLK), lambda j, t: (idx, 0, j)),
                  pl.BlockSpec((None, D_MODEL, FF_BLK), lambda j, t: (idx, 0, N_FF_BLK + j))],
        out_specs=pl.BlockSpec((tm, FF_BLK), lambda j, t: (t, j)),
        out_shape=jax.ShapeDtypeStruct((T, D_FF), BF16),
        scratch_shapes=[pltpu.VMEM((D_MODEL, FF_BLK), BF16), pltpu.VMEM((D_MODEL, FF_BLK), BF16)],
        compiler_params=_cp(2, 48),
        name="ffn_up",
    )(h2, w_gate_up, w_gate_up)


def _ffn_down_body(a_ref, w_ref, x_ref, o_ref, wbf):
    @pl.when(pl.program_id(1) == 0)
    def _():
        wbf[...] = w_ref[...].astype(BF16)

    o_ref[...] = x_ref[...] + _dot(a_ref[...], wbf[...])


def ffn_down(act, w_down, idx, x1):
    tm, tn = 512, 512
    return pl.pallas_call(
        _ffn_down_body,
        grid=(D_MODEL // tn, T // tm),
        in_specs=[pl.BlockSpec((tm, D_FF), lambda j, t: (t, 0)),
                  pl.BlockSpec((None, D_FF, tn), lambda j, t: (idx, 0, j)),
                  pl.BlockSpec((tm, tn), lambda j, t: (t, j))],
        out_specs=pl.BlockSpec((tm, tn), lambda j, t: (t, j)),
        out_shape=jax.ShapeDtypeStruct((T, D_MODEL), F32),
        scratch_shapes=[pltpu.VMEM((D_FF, tn), BF16)],
        compiler_params=_cp(2, 56),
        name="ffn_down",
    )(act, w_down, x1)


def _ple_body(h_ref, pp_ref, ps_ref, x_ref, wg_ref, wp_ref, *rest, last):
    t = pl.program_id(0)
    if last:
        yp_ref, ys_ref, wg, wp = rest
    else:
        gn_ref, x3_ref, hn_ref, wg, wp = rest

    @pl.when(t == 0)
    def _():
        wg[...] = wg_ref[...].astype(BF16)
        wp[...] = wp_ref[...].astype(BF16)

    gate = jax.nn.sigmoid(_dot(h_ref[...], wg[...]))
    p = _load_rows((pp_ref, ps_ref), t).astype(BF16)
    x3 = x_ref[...] + _dot(p, wp[...]) * gate
    if last:
        pb = TP // x3.shape[0]

        @pl.when(t < pb)
        def _():
            yp_ref[...] = x3

        @pl.when(t >= pb)
        def _():
            ys_ref[...] = x3
    else:
        x3_ref[...] = x3
        hn_ref[...] = _rms(x3, gn_ref[...]).astype(hn_ref.dtype)


def ple(h3, p_prompt, p_sample, x2, w_ple_gate, w_ple, layer, g_next):
    tm = 256
    last = g_next is None
    pb = TP // tm
    row = lambda t: (t, 0)
    prow = lambda t: (jnp.minimum(t, pb - 1), 0)
    srow = lambda t: (jnp.maximum(t - pb, 0), 0)
    in_specs = [pl.BlockSpec((tm, D_MODEL), row),
                pl.BlockSpec((None, tm, PLE_DIM), lambda t: (layer, jnp.minimum(t, pb - 1), 0)),
                pl.BlockSpec((None, tm, PLE_DIM), lambda t: (layer, jnp.maximum(t - pb, 0), 0)),
                pl.BlockSpec((tm, D_MODEL), row),
                pl.BlockSpec((None, D_MODEL, D_MODEL), lambda t: (layer, 0, 0), pipeline_mode=pl.Buffered(1)),
                pl.BlockSpec((None, PLE_DIM, D_MODEL), lambda t: (layer, 0, 0), pipeline_mode=pl.Buffered(1))]
    args = [h3, p_prompt.reshape(DEPTH, TP, PLE_DIM), p_sample.reshape(DEPTH, TS, PLE_DIM), x2, w_ple_gate, w_ple]
    if last:
        out_specs = [pl.BlockSpec((tm, D_MODEL), prow), pl.BlockSpec((tm, D_MODEL), srow)]
        out_shape = [jax.ShapeDtypeStruct((TP, D_MODEL), F32), jax.ShapeDtypeStruct((TS, D_MODEL), F32)]
    else:
        in_specs.append(pl.BlockSpec((1, D_MODEL), lambda t: (0, 0)))
        args.append(g_next.reshape(1, D_MODEL))
        out_specs = [pl.BlockSpec((tm, D_MODEL), row), pl.BlockSpec((tm, D_MODEL), row)]
        out_shape = [jax.ShapeDtypeStruct((T, D_MODEL), F32), jax.ShapeDtypeStruct((T, D_MODEL), BF16)]
    return pl.pallas_call(
        functools.partial(_ple_body, last=last),
        grid=(T // tm,),
        in_specs=in_specs,
        out_specs=out_specs,
        out_shape=out_shape,
        scratch_shapes=[pltpu.VMEM((D_MODEL, D_MODEL), BF16), pltpu.VMEM((PLE_DIM, D_MODEL), BF16)],
        compiler_params=_cp(1, 56),
        name="ple",
    )(*args)


def _router_body(lg_ref, meta_ref, cnt_ref, carry):
    @pl.when(pl.program_id(0) == 0)
    def _():
        carry[...] = jnp.zeros_like(carry)

    lg = lg_ref[...]
    tm = lg.shape[0]
    lane = lax.broadcasted_iota(I32, (tm, LANES), 1)
    lane_f = lane.astype(F32)
    big = float(LANES)
    l1 = jnp.where(lane < N_EXPERTS, lg, -jnp.inf)
    m1 = jnp.max(l1, axis=1, keepdims=True)
    i1 = jnp.min(jnp.where(l1 == m1, lane_f, big), axis=1, keepdims=True)
    oh1 = lane_f == i1
    l2 = jnp.where(oh1, -jnp.inf, l1)
    m2 = jnp.max(l2, axis=1, keepdims=True)
    i2 = jnp.min(jnp.where(l2 == m2, lane_f, big), axis=1, keepdims=True)
    oh2 = lane_f == i2
    e = jnp.exp(m2 - m1)
    w1 = 1.0 / (1.0 + e)
    w2 = e / (1.0 + e)
    oh = jnp.where(oh1 | oh2, 1.0, 0.0)
    r = lax.broadcasted_iota(I32, (tm, tm), 0)
    c = lax.broadcasted_iota(I32, (tm, tm), 1)
    tri = jnp.where(c < r, 1.0, 0.0).astype(BF16)
    before = _dot(tri, oh.astype(BF16)) + carry[0:1, :]
    r1 = jnp.sum(jnp.where(oh1, before, 0.0), axis=1, keepdims=True)
    r2 = jnp.sum(jnp.where(oh2, before, 0.0), axis=1, keepdims=True)
    meta = jnp.where(lane == 0, i1, jnp.where(lane == 1, i2, jnp.where(lane == 2, w1,
           jnp.where(lane == 3, w2, jnp.where(lane == 4, r1, jnp.where(lane == 5, r2, 0.0))))))
    meta_ref[...] = meta
    total = carry[0:1, :] + jnp.sum(oh, axis=0, keepdims=True)
    carry[...] = jnp.broadcast_to(total, carry.shape)
    cnt_ref[...] = jnp.broadcast_to(total, cnt_ref.shape)


def route(logits):
    tm = 512
    return pl.pallas_call(
        _router_body,
        grid=(T // tm,),
        in_specs=[pl.BlockSpec((tm, LANES), lambda t: (t, 0))],
        out_specs=[pl.BlockSpec((tm, LANES), lambda t: (t, 0)),
                   pl.BlockSpec((8, LANES), lambda t: (0, 0))],
        out_shape=[jax.ShapeDtypeStruct((T, LANES), F32), jax.ShapeDtypeStruct((8, LANES), F32)],
        scratch_shapes=[pltpu.VMEM((8, LANES), F32)],
        compiler_params=_cp(1, 32),
        name="moe_route",
    )(logits)


def moe_plan(meta, cnt_rows):
    i1 = meta[:, 0].astype(I32)
    i2 = meta[:, 1].astype(I32)
    r1 = meta[:, 4].astype(I32)
    r2 = meta[:, 5].astype(I32)
    cnt = cnt_rows[0, :N_EXPERTS].astype(I32)
    tiles_e = (cnt + MOE_TILE - 1) // MOE_TILE
    cum_tiles = jnp.cumsum(tiles_e)
    pad_off = (cum_tiles - tiles_e) * MOE_TILE
    n_tiles = cum_tiles[-1:]
    e_ids = jnp.arange(N_EXPERTS, dtype=I32)
    pick = lambda idx: jnp.sum(jnp.where(idx[:, None] == e_ids[None, :], pad_off[None, :], 0), axis=1)
    pos1 = pick(i1) + r1
    pos2 = pick(i2) + r2
    last_e = jnp.max(jnp.where(cnt > 0, e_ids, 0))
    tile_ids = jnp.arange(MOE_TILES, dtype=I32)
    tile_e = jnp.minimum(jnp.sum(tile_ids[:, None] >= cum_tiles[None, :], axis=1).astype(I32), last_e)
    pad_rows = (pad_off + cnt)[:, None] + jnp.arange(MOE_TILE - 1, dtype=I32)[None, :]
    pad_keys = jnp.where(pad_rows < (pad_off + tiles_e * MOE_TILE)[:, None], pad_rows, 2 * MOE_ROWS).reshape(-1)
    tok = jnp.arange(T, dtype=I32)
    keys = jnp.concatenate([pos1, pos2, pad_keys])
    vals = jnp.concatenate([tok, tok, jnp.zeros((MOE_PAD,), I32)])
    row_tok = lax.sort((keys, vals), num_keys=1)[1][:MOE_ROWS]
    return pos1, pos2, tile_e, n_tiles, row_tok


def _row_copy(src_hbm, dst_vmem, sem, src_row, dst_row):
    return pltpu.make_async_copy(src_hbm.at[pl.ds(src_row, 1)], dst_vmem.at[pl.ds(dst_row, 1)], sem)


def _start_rows(idx_ref, src_hbm, dst_vmem, sem, stride=1):
    n = dst_vmem.shape[0]

    def body(i, carry):
        r = (i * stride) & (n - 1)
        _row_copy(src_hbm, dst_vmem, sem, idx_ref[0, 0, r], r).start()
        return carry

    lax.fori_loop(0, n, body, 0, unroll=8)


def _wait_rows(src_hbm, dst_vmem, sem):
    def body(r, carry):
        _row_copy(src_hbm, dst_vmem, sem, 0, r).wait()
        return carry

    lax.fori_loop(0, dst_vmem.shape[0], body, 0, unroll=8)


def _zero_dead_tile(nt_ref, r, o_ref):
    @pl.when(r >= nt_ref[0])
    def _():
        o_ref[...] = jnp.zeros_like(o_ref)


def _dispatch_body(nt_ref, tok_ref, nxt_ref, h_hbm, o_ref, buf, sem):
    r = pl.program_id(0)
    slot = r % 2

    @pl.when(r == 0)
    def _():
        _start_rows(tok_ref, h_hbm, buf.at[0], sem.at[0], stride=GATHER_STRIDE)

    @pl.when(r + 1 < nt_ref[0])
    def _():
        _start_rows(nxt_ref, h_hbm, buf.at[1 - slot], sem.at[1 - slot], stride=GATHER_STRIDE)

    @pl.when(r < nt_ref[0])
    def _():
        _wait_rows(h_hbm, buf.at[slot], sem.at[slot])
        o_ref[...] = buf[slot].astype(o_ref.dtype)

    _zero_dead_tile(nt_ref, r, o_ref)


def moe_dispatch(n_tiles, row_tok, h2f):
    smem = lambda off: pl.BlockSpec((1, 1, MOE_TILE),
                                    lambda r, nt: (jnp.minimum(r + off, MOE_TILES - 1), 0, 0),
                                    memory_space=pltpu.SMEM)
    tok3 = row_tok.reshape(MOE_TILES, 1, MOE_TILE)
    return pl.pallas_call(
        _dispatch_body,
        grid_spec=pltpu.PrefetchScalarGridSpec(
            num_scalar_prefetch=1,
            grid=(MOE_TILES,),
            in_specs=[smem(0), smem(1), pl.BlockSpec(memory_space=pl.ANY)],
            out_specs=pl.BlockSpec((MOE_TILE, D_MODEL), lambda r, nt: (r, 0)),
            scratch_shapes=[pltpu.VMEM((2, MOE_TILE, D_MODEL), F32), pltpu.SemaphoreType.DMA((2,))]),
        out_shape=jax.ShapeDtypeStruct((MOE_ROWS, D_MODEL), BF16),
        compiler_params=_cp(1, 32),
        name="moe_dispatch",
    )(n_tiles, tok3, tok3, h2f)


def _expert_changed(te_ref, r):
    return jnp.logical_or(r == 0, te_ref[r] != te_ref[jnp.maximum(r - 1, 0)])


def _moe_up_body(te_ref, nt_ref, a_ref, wg_ref, wu_ref, o_ref, wg, wu):
    r = pl.program_id(1)

    @pl.when(r < nt_ref[0])
    def _():
        @pl.when(_expert_changed(te_ref, r))
        def _():
            wg[...] = wg_ref[...].astype(BF16)
            wu[...] = wu_ref[...].astype(BF16)

        a = a_ref[...]
        g = _dot(a, wg[...])
        o_ref[...] = (jax.nn.silu(g) * _dot(a, wu[...])).astype(o_ref.dtype)

    _zero_dead_tile(nt_ref, r, o_ref)


def moe_up(tile_e, n_tiles, xs, w_gate_up, idx):
    live = lambda j, r, te, nt: (jnp.minimum(r, nt[0] - 1), 0)
    return pl.pallas_call(
        _moe_up_body,
        grid_spec=pltpu.PrefetchScalarGridSpec(
            num_scalar_prefetch=2,
            grid=(N_FF_BLK, MOE_TILES),
            in_specs=[pl.BlockSpec((MOE_TILE, D_MODEL), live),
                      pl.BlockSpec((None, None, D_MODEL, FF_BLK),
                                   lambda j, r, te, nt: (idx, te[r], 0, j)),
                      pl.BlockSpec((None, None, D_MODEL, FF_BLK),
                                   lambda j, r, te, nt: (idx, te[r], 0, N_FF_BLK + j))],
            out_specs=pl.BlockSpec((MOE_TILE, FF_BLK), lambda j, r, te, nt: (r, j)),
            scratch_shapes=[pltpu.VMEM((D_MODEL, FF_BLK), BF16), pltpu.VMEM((D_MODEL, FF_BLK), BF16)]),
        out_shape=jax.ShapeDtypeStruct((MOE_ROWS, D_FF), BF16),
        compiler_params=_cp(2, 48),
        name="moe_up",
    )(tile_e, n_tiles, xs, w_gate_up, w_gate_up)


def _moe_down_body(te_ref, nt_ref, a_ref, w_ref, o_ref, wbf):
    r = pl.program_id(1)

    @pl.when(r < nt_ref[0])
    def _():
        @pl.when(_expert_changed(te_ref, r))
        def _():
            wbf[...] = w_ref[...].astype(BF16)

        o_ref[...] = _dot(a_ref[...], wbf[...])

    _zero_dead_tile(nt_ref, r, o_ref)


def moe_down(tile_e, n_tiles, act, w_down, idx):
    tn = 512
    return pl.pallas_call(
        _moe_down_body,
        grid_spec=pltpu.PrefetchScalarGridSpec(
            num_scalar_prefetch=2,
            grid=(D_MODEL // tn, MOE_TILES),
            in_specs=[pl.BlockSpec((MOE_TILE, D_FF), lambda j, r, te, nt: (jnp.minimum(r, nt[0] - 1), 0)),
                      pl.BlockSpec((None, None, D_FF, tn), lambda j, r, te, nt: (idx, te[r], 0, j))],
            out_specs=pl.BlockSpec((MOE_TILE, tn), lambda j, r, te, nt: (r, j)),
            scratch_shapes=[pltpu.VMEM((D_FF, tn), BF16)]),
        out_shape=jax.ShapeDtypeStruct((MOE_ROWS, D_MODEL), F32),
        compiler_params=_cp(2, 56),
        name="moe_down",
    )(tile_e, n_tiles, act, w_down)


def _combine_body(p1_ref, p2_ref, n1_ref, n2_ref, ys_hbm, meta_ref, x_ref, g_ref, x2_ref, h3_ref,
                  b1, b2, sem1, sem2):
    t = pl.program_id(0)
    slot = t % 2

    @pl.when(t == 0)
    def _():
        _start_rows(p1_ref, ys_hbm, b1.at[0], sem1.at[0])
        _start_rows(p2_ref, ys_hbm, b2.at[0], sem2.at[0])

    @pl.when(t + 1 < pl.num_programs(0))
    def _():
        _start_rows(n1_ref, ys_hbm, b1.at[1 - slot], sem1.at[1 - slot])
        _start_rows(n2_ref, ys_hbm, b2.at[1 - slot], sem2.at[1 - slot])

    _wait_rows(ys_hbm, b1.at[slot], sem1.at[slot])
    _wait_rows(ys_hbm, b2.at[slot], sem2.at[slot])
    meta = meta_ref[...]
    w1 = meta[:, 2:3]
    w2 = meta[:, 3:4]
    x2 = x_ref[...] + (w1 * b1[slot] + w2 * b2[slot])
    x2_ref[...] = x2
    h3_ref[...] = _rms(x2, g_ref[...]).astype(h3_ref.dtype)


def moe_combine(pos1, pos2, ys, meta, x1, g):
    tm = 256
    nblk = T // tm
    row = lambda t: (t, 0)
    smem = lambda off: pl.BlockSpec((1, 1, tm), lambda t: (jnp.minimum(t + off, nblk - 1), 0, 0),
                                    memory_space=pltpu.SMEM)
    p1 = pos1.reshape(nblk, 1, tm)
    p2 = pos2.reshape(nblk, 1, tm)
    return pl.pallas_call(
        _combine_body,
        grid=(nblk,),
        in_specs=[smem(0), smem(0), smem(1), smem(1),
                  pl.BlockSpec(memory_space=pl.ANY),
                  pl.BlockSpec((tm, LANES), row),
                  pl.BlockSpec((tm, D_MODEL), row),
                  pl.BlockSpec((1, D_MODEL), lambda t: (0, 0))],
        out_specs=[pl.BlockSpec((tm, D_MODEL), row), pl.BlockSpec((tm, D_MODEL), row)],
        out_shape=[jax.ShapeDtypeStruct((T, D_MODEL), F32), jax.ShapeDtypeStruct((T, D_MODEL), BF16)],
        scratch_shapes=[pltpu.VMEM((2, tm, D_MODEL), F32), pltpu.VMEM((2, tm, D_MODEL), F32),
                        pltpu.SemaphoreType.DMA((2,)), pltpu.SemaphoreType.DMA((2,))],
        compiler_params=_cp(1, 48),
        name="moe_combine",
    )(p1, p2, p1, p2, ys, meta, x1, g.reshape(1, D_MODEL))


def _last_rows(a, batch, seq, n):
    return a.reshape(batch, seq, a.shape[-1])[:, seq - n:]


def kernel(x_prompt, x_sample, cache_k, cache_v, state_pool, p_prompt, p_sample, norm_mix, w_in, q_norm, k_norm, sinks, pool_w, pool_scale, w_attn_br, w_pool_br, w_out, norm_ffn, ffn_w_gate_up, ffn_w_down, moe_router, moe_w_gate_up, moe_w_down, norm_ple, w_ple, w_ple_gate):
    x_parts = (x_prompt.reshape(TP, D_MODEL), x_sample.reshape(TS, D_MODEL))
    cos_t, sin_t = rope_tables()
    outs = {k: [] for k in ("kp", "vp", "pp", "ks", "vs", "ps")}
    h = rmsnorm_bf16(x_parts, norm_mix[0])

    for l in range(DEPTH):
        q_gain = jnp.tile(q_norm[l], ATTN_W // HEAD_DIM).reshape(1, ATTN_W)
        k_gain = jnp.tile(k_norm[l], KV_W // HEAD_DIM).reshape(1, KV_W)
        q = project(h, w_in, l, 0, ATTN_W, 512, 512, BF16, qk=(q_gain, cos_t, sin_t))
        k = project(h, w_in, l, ATTN_W // KV_W, KV_W, KV_W, 512, F32, qk=(k_gain, cos_t, sin_t))
        v = project(h, w_in, l, (ATTN_W + KV_W) // KV_W, KV_W, KV_W, 1024, F32)
        u = project(h, w_in, l, (ATTN_W + 2 * KV_W) // 512, POOL_W, 512, 1024, F32)

        ck = cache_k[l].reshape(DEC_BATCH * WINDOW, KV_W)
        cv = cache_v[l].reshape(DEC_BATCH * WINDOW, KV_W)
        o_p = attention_prompt(sinks[l], q, k, v)
        o_s = attention_sample(sinks[l], q, k, v, ck, cv)

        hist = jnp.pad(state_pool[l], ((0, 0), (HALO - POOL_HIST, 0), (0, 0))).reshape(DEC_BATCH * HALO, POOL_W)
        scale = pool_scale[l].reshape(1, POOL_W)
        y_p = pool_prompt(u, pool_w, scale, l)
        y_s = pool_sample(u, hist, pool_w, scale, l)

        mixed = branch_mix(h, o_p, o_s, y_p, y_s, w_in, w_attn_br, w_pool_br, l)

        if l % 2 == 0:
            x1, h2 = out_proj(mixed, x_parts, w_out, norm_ffn[l], l)
            act = ffn_up(h2, ffn_w_gate_up, l // 2)
            x2 = ffn_down(act, ffn_w_down, l // 2, x1)
            h3 = rmsnorm_bf16((x2,), norm_ple[l])
        else:
            router_w = jnp.pad(moe_router[l // 2], ((0, 0), (0, LANES - N_EXPERTS)))
            x1, h2f, logits = out_proj(mixed, x_parts, w_out, norm_ffn[l], l, router_w=router_w)
            meta, cnt_rows = route(logits)
            pos1, pos2, tile_e, n_tiles, row_tok = moe_plan(meta, cnt_rows)
            xs = moe_dispatch(n_tiles, row_tok, h2f)
            act = moe_up(tile_e, n_tiles, xs, moe_w_gate_up, l // 2)
            ys = moe_down(tile_e, n_tiles, act, moe_w_down, l // 2)
            x2, h3 = moe_combine(pos1, pos2, ys, meta, x1, norm_ple[l])

        if l + 1 < DEPTH:
            x, h = ple(h3, p_prompt, p_sample, x2, w_ple_gate, w_ple, l, norm_mix[l + 1])
            x_parts = (x,)
        else:
            y_prompt, y_sample = ple(h3, p_prompt, p_sample, x2, w_ple_gate, w_ple, l, None)

        kv_shape = (N_KV_HEADS, HEAD_DIM)
        outs["kp"].append(_last_rows(k[:TP], BATCH, SEQ, WINDOW).reshape(BATCH, WINDOW, *kv_shape))
        outs["vp"].append(_last_rows(v[:TP], BATCH, SEQ, WINDOW).reshape(BATCH, WINDOW, *kv_shape))
        ksn = k[TP:].reshape(DEC_BATCH, DEC_SEQ, *kv_shape)
        vsn = v[TP:].reshape(DEC_BATCH, DEC_SEQ, *kv_shape)
        outs["ks"].append(jnp.concatenate([cache_k[l][:, DEC_SEQ:], ksn], axis=1))
        outs["vs"].append(jnp.concatenate([cache_v[l][:, DEC_SEQ:], vsn], axis=1))
        outs["pp"].append(_last_rows(u[:TP], BATCH, SEQ, POOL_HIST))
        us = u[TP:].reshape(DEC_BATCH, DEC_SEQ, POOL_W)
        outs["ps"].append(jnp.concatenate([state_pool[l], us], axis=1)[:, -POOL_HIST:])

    return (y_prompt.reshape(BATCH, SEQ, D_MODEL), y_sample.reshape(DEC_BATCH, DEC_SEQ, D_MODEL),
            jnp.stack(outs["kp"]), jnp.stack(outs["vp"]), jnp.stack(outs["pp"]),
            jnp.stack(outs["ks"]), jnp.stack(outs["vs"]), jnp.stack(outs["ps"]))
```
